```python
import jax, jax.numpy as jnp
from jax import lax
import numpy as np

D_MODEL = 4096
BATCH = 4
SEQ = 2048
DEPTH = 1
DEC_BATCH = 128
DEC_SEQ = 4
PAST_LEN = 2048
PAGE_SIZE = 128

N_META = 16
SB_HEADS = 16
HEAD_DIM = 128
SB_WIDTH = SB_HEADS * HEAD_DIM
POOL_WIDTH = D_MODEL - SB_WIDTH
POOL_WINDOWS = (2, 4, 8, 16)
N_POOL_GROUPS = len(POOL_WINDOWS)
POOL_GROUP_DIM = POOL_WIDTH // N_POOL_GROUPS
POOL_BUF = max(POOL_WINDOWS) - 1
MIX_WIDTH = SB_WIDTH + POOL_WIDTH
IN_WIDTH = 3 * SB_WIDTH + POOL_WIDTH
D_FF = 4 * D_MODEL
Q_BLOCK = 128
EPS = 1e-6
SB_BIAS_INIT = -7.0

kernel_name = "hymba_stickbreak_pool_decode_step"


def rms_norm(x, g):
    x32 = x.astype(jnp.float32)
    y = x32 * lax.rsqrt(jnp.mean(x32 * x32, axis=-1, keepdims=True) + EPS)
    return (y * g.astype(jnp.float32)).astype(x.dtype)


def stick_breaking_attend(q, k, v, q_pos, k_pos, bias):
    z = jnp.einsum("bqhd,bshd->bhqs", q.astype(jnp.float32), k.astype(jnp.float32)) * (HEAD_DIM ** -0.5)
    z = z + bias.astype(jnp.float32)[None, :, None, None]
    valid = (k_pos[None, :] < q_pos[:, None]) & (k_pos[None, :] >= 0)
    log_one_minus = jnp.where(valid, jax.nn.log_sigmoid(-z), 0.0)
    later = lax.cumsum(log_one_minus, axis=3, reverse=True) - log_one_minus
    weights = jnp.where(valid, jnp.exp(jax.nn.log_sigmoid(z) + later), 0.0)
    out = jnp.einsum("bhqs,bshd->bqhd", weights, v.astype(jnp.float32))
    return out.astype(q.dtype)


def causal_multiscale_pool(p_ext, pos):
    L = pos.shape[0]
    p32 = p_ext.astype(jnp.float32)
    cs = jnp.concatenate([jnp.zeros_like(p32[:, :1]), lax.cumsum(p32, axis=1)], axis=1)
    end = cs[:, POOL_BUF + 1:]
    x_new = p32[:, POOL_BUF:]
    means = []
    for g, w in enumerate(POOL_WINDOWS):
        lo, hi = g * POOL_GROUP_DIM, (g + 1) * POOL_GROUP_DIM
        start = cs[:, POOL_BUF + 1 - w: POOL_BUF + 1 - w + L, lo:hi]
        count = jnp.minimum(pos + 1, w).astype(jnp.float32)[None, :, None]
        means.append((end[..., lo:hi] - start) / count)
    return jnp.concatenate(means, axis=-1) - x_new


def pool_mixer(p_ext, pos, w_pool, pool_scale):
    d = causal_multiscale_pool(p_ext, pos)
    B, L, _ = d.shape
    d = d.reshape(B, L, N_POOL_GROUPS, POOL_GROUP_DIM)
    out = jnp.einsum("blgc,gce->blge", d, w_pool.astype(jnp.float32)).reshape(B, L, POOL_WIDTH)
    return (out * pool_scale.astype(jnp.float32)).astype(p_ext.dtype)


def in_projection(x, g_pre, w_in):
    B, L, _ = x.shape
    proj = rms_norm(x, g_pre) @ w_in
    q, k, v, p = jnp.split(proj, [SB_WIDTH, 2 * SB_WIDTH, 3 * SB_WIDTH], axis=-1)
    q = q.reshape(B, L, SB_HEADS, HEAD_DIM)
    k = k.reshape(B, L, SB_HEADS, HEAD_DIM)
    v = v.reshape(B, L, SB_HEADS, HEAD_DIM)
    return q, k, v, p


def finish_layer(x, attn_out, pool_out, w_out, g_mix_post, g_mlp_pre, w_up, w_down, g_mlp_post):
    B, L, _ = x.shape
    mixed = jnp.concatenate([attn_out.reshape(B, L, SB_WIDTH), pool_out], axis=-1) @ w_out
    h = x + rms_norm(mixed, g_mix_post)
    u = jax.nn.relu(rms_norm(h, g_mlp_pre) @ w_up)
    return h + rms_norm((u * u) @ w_down, g_mlp_post)


def setup_inputs(seed: int = 0) -> dict:
    key = jax.random.key(seed)
    ks = jax.random.split(key, 19)
    f32 = jnp.float32
    n_pages = PAST_LEN // PAGE_SIZE
    n_used = DEC_BATCH * n_pages
    n_phys = n_used + n_used // 4
    page_table = jax.random.permutation(ks[0], n_phys)[:n_used].reshape(DEC_BATCH, n_pages).astype(jnp.int32)

    def nrm(k, shape, scale=1.0):
        return scale * jax.random.normal(k, shape, f32)

    def gain(k, shape):
        return 1.0 + 0.02 * jax.random.normal(k, shape, f32)

    return {
        "x_prompt": nrm(ks[1], (BATCH, SEQ, D_MODEL)),
        "x_sample": nrm(ks[2], (DEC_BATCH, DEC_SEQ, D_MODEL)),
        "cache_k": nrm(ks[3], (DEPTH, n_phys, PAGE_SIZE, SB_HEADS, HEAD_DIM)),
        "cache_v": nrm(ks[4], (DEPTH, n_phys, PAGE_SIZE, SB_HEADS, HEAD_DIM)),
        "state_pool": nrm(ks[5], (DEPTH, DEC_BATCH, POOL_BUF, POOL_WIDTH)),
        "page_table": page_table,
        "meta": nrm(ks[6], (N_META, D_MODEL)),
        "g_mix_pre": gain(ks[7], (DEPTH, D_MODEL)),
        "w_in": nrm(ks[8], (DEPTH, D_MODEL, IN_WIDTH), D_MODEL ** -0.5),
        "sb_bias": SB_BIAS_INIT + 0.1 * jax.random.normal(ks[17], (DEPTH, SB_HEADS), f32),
        "w_pool": nrm(ks[9], (DEPTH, N_POOL_GROUPS, POOL_GROUP_DIM, POOL_GROUP_DIM), POOL_GROUP_DIM ** -0.5),
        "pool_scale": gain(ks[10], (DEPTH, POOL_WIDTH)),
        "w_out": nrm(ks[11], (DEPTH, MIX_WIDTH, D_MODEL), MIX_WIDTH ** -0.5),
        "g_mix_post": gain(ks[12], (DEPTH, D_MODEL)),
        "g_mlp_pre": gain(ks[13], (DEPTH, D_MODEL)),
        "w_up": nrm(ks[14], (DEPTH, D_MODEL, D_FF), D_MODEL ** -0.5),
        "w_down": nrm(ks[15], (DEPTH, D_FF, D_MODEL), D_FF ** -0.5),
        "g_mlp_post": gain(ks[16], (DEPTH, D_MODEL)),
    }


def reference(x_prompt, x_sample, cache_k, cache_v, state_pool, page_table, meta, g_mix_pre, w_in, sb_bias,
              w_pool, pool_scale, w_out, g_mix_post, g_mlp_pre, w_up, w_down, g_mlp_post):
    B, S = x_prompt.shape[:2]
    DB, DS = x_sample.shape[:2]
    n_pages = PAST_LEN // PAGE_SIZE
    T = N_META + S
    pad_front = (-N_META) % Q_BLOCK
    n_blocks = (T + pad_front) // Q_BLOCK
    pos_prompt = jnp.arange(T, dtype=jnp.int32)
    k_pos_pad = jnp.arange(T + pad_front, dtype=jnp.int32) - pad_front
    q_pos_blocks = k_pos_pad.reshape(n_blocks, Q_BLOCK)
    pos_sample = PAST_LEN + jnp.arange(DS, dtype=jnp.int32)
    k_pos_sample = jnp.arange(PAST_LEN + DS, dtype=jnp.int32)

    xp = jnp.concatenate([jnp.broadcast_to(meta[None].astype(x_prompt.dtype), (B, N_META, D_MODEL)), x_prompt], axis=1)
    xs = x_sample
    k_p, v_p, pool_p, k_s, v_s, pool_s = [], [], [], [], [], []
    for layer in range(DEPTH):
        mlp_args = (w_out[layer], g_mix_post[layer], g_mlp_pre[layer], w_up[layer], w_down[layer], g_mlp_post[layer])
        bias = sb_bias[layer]

        q, k, v, p = in_projection(xp, g_mix_pre[layer], w_in[layer])
        padcfg = ((0, 0), (pad_front, 0), (0, 0), (0, 0))
        qb = jnp.pad(q, padcfg).reshape(B, n_blocks, Q_BLOCK, SB_HEADS, HEAD_DIM).transpose(1, 0, 2, 3, 4)
        kp = jnp.pad(k, padcfg)
        vp = jnp.pad(v, padcfg)
        ob = lax.map(lambda a: stick_breaking_attend(a[0], kp, vp, a[1], k_pos_pad, bias), (qb, q_pos_blocks))
        attn = ob.transpose(1, 0, 2, 3, 4).reshape(B, n_blocks * Q_BLOCK, SB_HEADS, HEAD_DIM)[:, pad_front:]
        p_ext = jnp.concatenate([jnp.zeros((B, POOL_BUF, POOL_WIDTH), p.dtype), p], axis=1)
        pool = pool_mixer(p_ext, pos_prompt, w_pool[layer], pool_scale[layer])
        k_p.append(k)
        v_p.append(v)
        pool_p.append(p[:, -POOL_BUF:])
        xp = finish_layer(xp, attn, pool, *mlp_args)

        q, k, v, p = in_projection(xs, g_mix_pre[layer], w_in[layer])
        k_past = cache_k[layer][page_table[:, :n_pages]].reshape(DB, PAST_LEN, SB_HEADS, HEAD_DIM).astype(k.dtype)
        v_past = cache_v[layer][page_table[:, :n_pages]].reshape(DB, PAST_LEN, SB_HEADS, HEAD_DIM).astype(v.dtype)
        attn = stick_breaking_attend(q, jnp.concatenate([k_past, k], axis=1),
                                     jnp.concatenate([v_past, v], axis=1), pos_sample, k_pos_sample, bias)
        p_ext = jnp.concatenate([state_pool[layer].astype(p.dtype), p], axis=1)
        pool = pool_mixer(p_ext, pos_sample, w_pool[layer], pool_scale[layer])
        k_s.append(k)
        v_s.append(v)
        pool_s.append(p_ext[:, -POOL_BUF:])
        xs = finish_layer(xs, attn, pool, *mlp_args)

    y_prompt = xp[:, N_META:]
    return (y_prompt, xs, jnp.stack(k_p), jnp.stack(v_p), jnp.stack(pool_p),
            jnp.stack(k_s), jnp.stack(v_s), jnp.stack(pool_s))
```

```python
import functools

import numpy as np
import jax
import jax.numpy as jnp
from jax import lax
from jax.experimental import pallas as pl
from jax.experimental.pallas import tpu as pltpu

F32 = jnp.float32
BF16 = jnp.bfloat16

HEAD_DIM = 128
N_META = 16
POOL_WINDOWS = (2, 4, 8, 16)
POOL_BUF = max(POOL_WINDOWS) - 1
EPS = 1e-6
LANES = 128
SUBLANES = 8
VMEM_LIMIT_BYTES = 56 * 1024 * 1024


def _tile(n, pref):
    if n <= pref:
        return n
    t = pref
    while n % t:
        t //= 2
    return t


def _params(*sem):
    return pltpu.CompilerParams(dimension_semantics=sem, vmem_limit_bytes=VMEM_LIMIT_BYTES)


def _dot(a, b):
    return jnp.dot(a, b, preferred_element_type=F32)


def _dot_nt(a, b):
    return lax.dot_general(a, b, (((1,), (1,)), ((), ())), preferred_element_type=F32)


def _split_bf16(x):
    hi = x.astype(BF16)
    lo = (x - hi.astype(F32)).astype(BF16)
    return hi, lo


def _rms(x, g):
    ms = jnp.mean(x * x, axis=-1, keepdims=True)
    return x * lax.rsqrt(ms + EPS) * g


def _for_row_chunks(n_rows, chunk, fn):
    chunk = min(chunk, n_rows)
    assert n_rows % chunk == 0

    def body(c, carry):
        fn(pl.ds(pl.multiple_of(c * chunk, chunk), chunk))
        return carry

    lax.fori_loop(0, n_rows // chunk, body, 0)


def _inproj_kernel(x_ref, g_ref, wq_ref, wk_ref, wv_ref, wp_ref, q_ref, k_ref, v_ref, p_ref, xn_ref):
    @pl.when(pl.program_id(1) == 0)
    def _():
        g = g_ref[...]

        def norm(rows):
            xn_ref[rows, :] = _rms(x_ref[rows, :], g).astype(BF16)

        _for_row_chunks(x_ref.shape[0], 16, norm)

    xn = xn_ref[...]
    q_ref[...] = _dot(xn, wq_ref[...]).astype(q_ref.dtype)
    k_ref[...] = _dot(xn, wk_ref[...])
    v_ref[...] = _dot(xn, wv_ref[...])
    p_ref[...] = _dot(xn, wp_ref[...])


def _in_projection(x, g, w_in, sbw, pw, tm_pref=512, tn_pref=256):
    m, d = x.shape
    tm = _tile(m, tm_pref)
    tn = _tile(min(sbw, pw), tn_pref)
    assert sbw % tn == 0 and pw % tn == 0 and sbw == pw
    nj = sbw // tn

    def wspec(c):
        return pl.BlockSpec((d, tn), lambda i, j: (0, j + c * nj))

    ospec = pl.BlockSpec((tm, tn), lambda i, j: (i, j))
    return pl.pallas_call(
        _inproj_kernel,
        grid=(m // tm, nj),
        in_specs=[pl.BlockSpec((tm, d), lambda i, j: (i, 0)),
                  pl.BlockSpec((1, d), lambda i, j: (0, 0)),
                  wspec(0), wspec(1), wspec(2), wspec(3)],
        out_specs=[ospec, ospec, ospec, ospec],
        out_shape=[jax.ShapeDtypeStruct((m, sbw), BF16),
                   jax.ShapeDtypeStruct((m, sbw), F32),
                   jax.ShapeDtypeStruct((m, sbw), F32),
                   jax.ShapeDtypeStruct((m, pw), F32)],
        scratch_shapes=[pltpu.VMEM((tm, d), BF16)],
        compiler_params=_params("parallel", "arbitrary"),
        name="in_projection",
    )(x, g, w_in, w_in, w_in, w_in)


def _suffix_matrix(tk):
    j = np.arange(tk)[:, None]
    s = np.arange(tk)[None, :]
    u = (j > s).astype(np.float32)
    return jnp.asarray(np.concatenate([u, np.ones((tk, LANES), np.float32)], axis=1), BF16)


def _sb_weights(z, valid, u1, carry):
    tk = z.shape[1]
    l = -(jnp.maximum(z, 0.0) + jnp.log1p(jnp.exp(-jnp.abs(z))))
    if valid is not None:
        l = jnp.where(valid, l, 0.0)
    l_hi, l_lo = _split_bf16(l)
    lu = _dot(l_hi, u1) + _dot(l_lo, u1)
    later = lu[:, :tk] + jnp.concatenate([carry] * (tk // LANES), axis=1)
    w = jnp.exp(z + l + later)
    if valid is not None:
        w = jnp.where(valid, w, 0.0)
    return w, carry + lu[:, tk:]


def _prompt_attn_kernel(bias_ref, q_ref, k_ref, v_ref, km_ref, vm_ref, u1_ref, u1m_ref, o_ref, kb_ref, vb_ref,
                        *, tq, scale):
    h = pl.program_id(1)
    qi = pl.program_id(2)

    @pl.when(qi == 0)
    def _():
        kb_ref[...] = k_ref[...].astype(BF16)
        vb_ref[...] = v_ref[...].astype(BF16)

    q = q_ref[...]
    bias = bias_ref[h]
    u1 = u1_ref[...]

    def block(kblk, vblk, valid, u, carry, acc):
        z = _dot_nt(q, kblk) * scale + bias
        w, carry = _sb_weights(z, valid, u, carry)
        return carry, acc + _dot(w.astype(BF16), vblk)

    carry = jnp.zeros((tq, LANES), F32)
    acc = jnp.zeros((tq, HEAD_DIM), F32)

    row = lax.broadcasted_iota(jnp.int32, (tq, tq), 0)
    col = lax.broadcasted_iota(jnp.int32, (tq, tq), 1)
    off = pl.multiple_of(qi * tq, tq)
    carry, acc = block(kb_ref[pl.ds(off, tq), :], vb_ref[pl.ds(off, tq), :], col < row, u1, carry, acc)

    def body(it, c):
        off = pl.multiple_of((qi - 1 - it) * tq, tq)
        return block(kb_ref[pl.ds(off, tq), :], vb_ref[pl.ds(off, tq), :], None, u1, *c)

    carry, acc = lax.fori_loop(0, qi, body, (carry, acc))

    colm = lax.broadcasted_iota(jnp.int32, (tq, LANES), 1)
    carry, acc = block(km_ref[...].astype(BF16), vm_ref[...].astype(BF16), colm < N_META, u1m_ref[...], carry, acc)
    o_ref[...] = acc.astype(o_ref.dtype)


def _prompt_attention(q, k, v, k_meta, v_meta, bias, batch, seq, tq_pref=256):
    m, sbw = q.shape
    nh = sbw // HEAD_DIM
    tq = _tile(seq, tq_pref)
    nq = seq // tq
    pad = ((0, LANES - N_META), (0, 0))
    km = jnp.pad(k_meta, pad)
    vm = jnp.pad(v_meta, pad)
    kern = functools.partial(_prompt_attn_kernel, tq=tq, scale=HEAD_DIM ** -0.5)
    qspec = pl.BlockSpec((tq, HEAD_DIM), lambda b, h, i: (b * nq + i, h))
    kvspec = pl.BlockSpec((seq, HEAD_DIM), lambda b, h, i: (b, h))
    mspec = pl.BlockSpec((LANES, HEAD_DIM), lambda b, h, i: (0, h))
    return pl.pallas_call(
        kern,
        grid=(batch, nh, nq),
        in_specs=[pl.BlockSpec(memory_space=pltpu.SMEM), qspec, kvspec, kvspec, mspec, mspec,
                  pl.BlockSpec((tq, tq + LANES), lambda b, h, i: (0, 0)),
                  pl.BlockSpec((LANES, 2 * LANES), lambda b, h, i: (0, 0))],
        out_specs=qspec,
        out_shape=jax.ShapeDtypeStruct((m, sbw), BF16),
        scratch_shapes=[pltpu.VMEM((seq, HEAD_DIM), BF16), pltpu.VMEM((seq, HEAD_DIM), BF16)],
        compiler_params=_params("parallel", "parallel", "arbitrary"),
        name="prompt_attention",
    )(bias, q, k, v, km, vm, _suffix_matrix(tq), _suffix_matrix(LANES))


def _sample_attn_kernel(pt_ref, q_ref, kn_ref, vn_ref, bias_ref, u1_ref, *rest, nh, group, scale):
    kc_refs = rest[:group]
    vc_refs = rest[group:2 * group]
    o_ref, carry_ref, acc_ref = rest[2 * group:]
    del pt_ref
    j = pl.program_id(1)
    rows = nh * SUBLANES
    page = kc_refs[0].shape[0]
    u1 = u1_ref[...]
    bias = bias_ref[...]
    row_t = lax.broadcasted_iota(jnp.int32, (rows, page), 0) % SUBLANES
    col = lax.broadcasted_iota(jnp.int32, (rows, page), 1)

    def attend(k_of_head, v_of_head, valid):
        z = jnp.concatenate(
            [_dot_nt(q_ref[h * SUBLANES:(h + 1) * SUBLANES, :], k_of_head(h)) for h in range(nh)], axis=0)
        z = z * scale + bias
        w, carry = _sb_weights(z, valid, u1, carry_ref[...])
        carry_ref[...] = carry
        w = w.astype(BF16)
        upd = jnp.concatenate(
            [_dot(w[h * SUBLANES:(h + 1) * SUBLANES, :], v_of_head(h)) for h in range(nh)], axis=0)
        acc_ref[...] += upd

    @pl.when(j == 0)
    def _():
        carry_ref[...] = jnp.zeros_like(carry_ref)
        acc_ref[...] = jnp.zeros_like(acc_ref)
        zpad = jnp.zeros((page - SUBLANES, HEAD_DIM), F32)

        def new_rows(ref):
            return lambda h: jnp.concatenate([ref[h * SUBLANES:(h + 1) * SUBLANES, :], zpad], axis=0).astype(BF16)

        attend(new_rows(kn_ref), new_rows(vn_ref), col < row_t)

    for c in range(group):
        kc, vc = kc_refs[c], vc_refs[c]
        attend(lambda h: kc[:, h, :].astype(BF16), lambda h: vc[:, h, :].astype(BF16), None)

    @pl.when(j == pl.num_programs(1) - 1)
    def _():
        o_ref[...] = acc_ref[...].astype(o_ref.dtype)


def _sample_attention(q, k_new, v_new, cache_k, cache_v, page_table, bias, group_pref=4):
    db, ds, sbw = q.shape
    nh = sbw // HEAD_DIM
    n_pages = page_table.shape[1]
    page = cache_k.shape[1]
    group = _tile(n_pages, group_pref)
    assert ds <= SUBLANES and page % LANES == 0 and page == LANES
    rows = nh * SUBLANES

    def head_major(a, dtype):
        a = a.reshape(db, ds, nh, HEAD_DIM).transpose(0, 2, 1, 3)
        a = jnp.pad(a, ((0, 0), (0, 0), (0, SUBLANES - ds), (0, 0)))
        return a.reshape(db, rows, HEAD_DIM).astype(dtype)

    qh = head_major(q, BF16)
    kn = head_major(k_new, F32)
    vn = head_major(v_new, F32)
    bias_rows = jnp.broadcast_to(jnp.repeat(bias.astype(F32), SUBLANES)[:, None], (rows, LANES))
    pt = page_table.reshape(-1).astype(jnp.int32)

    seq_spec = pl.BlockSpec((None, rows, HEAD_DIM), lambda n, j, pt: (n, 0, 0))

    def page_spec(c):
        def imap(n, j, pt):
            return (pt[n * n_pages + (n_pages - 1 - (j * group + c))], 0, 0, 0)
        return pl.BlockSpec((None, page, nh, HEAD_DIM), imap)

    kern = functools.partial(_sample_attn_kernel, nh=nh, group=group, scale=HEAD_DIM ** -0.5)
    out = pl.pallas_call(
        kern,
        grid_spec=pltpu.PrefetchScalarGridSpec(
            num_scalar_prefetch=1,
            grid=(db, n_pages // group),
            in_specs=[seq_spec, seq_spec, seq_spec,
                      pl.BlockSpec((rows, LANES), lambda n, j, pt: (0, 0)),
                      pl.BlockSpec((page, page + LANES), lambda n, j, pt: (0, 0))]
                     + [page_spec(c) for c in range(group)] * 2,
            out_specs=seq_spec,
            scratch_shapes=[pltpu.VMEM((rows, LANES), F32), pltpu.VMEM((rows, HEAD_DIM), F32)]),
        out_shape=jax.ShapeDtypeStruct((db, rows, HEAD_DIM), BF16),
        compiler_params=_params("parallel", "arbitrary"),
        name="sample_attention",
    )(pt, qh, kn, vn, bias_rows, _suffix_matrix(page),
      *([cache_k] * group), *([cache_v] * group))
    out = out.reshape(db, nh, SUBLANES, HEAD_DIM)[:, :, :ds].transpose(0, 2, 1, 3)
    return out.reshape(db, ds, sbw)


def _pool_tail(bm, ext, w_ref, s_ref, o_ref):
    hi, lo = _split_bf16(ext)
    d = _dot(bm, hi) + _dot(bm, lo)
    o_ref[...] = (_dot(d.astype(BF16), w_ref[...]) * s_ref[...]).astype(o_ref.dtype)


def _pool_prompt_kernel(p_ref, halo_ref, meta_ref, b_ref, w_ref, s_ref, o_ref, *, tiles_per_seq):
    first = (pl.program_id(0) % tiles_per_seq) == 0
    halo = jnp.where(first, meta_ref[...], halo_ref[...])
    _pool_tail(b_ref[...], jnp.concatenate([halo, p_ref[...]], axis=0), w_ref, s_ref, o_ref)


def _pool_sample_kernel(e_ref, b_ref, w_ref, s_ref, o_ref):
    _pool_tail(b_ref[...], e_ref[...], w_ref, s_ref, o_ref)


def _band_matrices(n_out, n_halo):
    b = np.zeros((len(POOL_WINDOWS), n_out, n_halo + n_out), np.float32)
    for g, w in enumerate(POOL_WINDOWS):
        for r in range(n_out):
            b[g, r, n_halo + r - w + 1:n_halo + r + 1] += 1.0 / w
            b[g, r, n_halo + r] -= 1.0
    return jnp.asarray(b, BF16)


def _pool_prompt(p, p_meta, w_pool, pool_scale, seq, tm_pref=256):
    m, pw = p.shape
    ng, gd, _ = w_pool.shape
    tm = _tile(seq, tm_pref)
    assert tm % LANES == 0
    hb = tm // LANES
    meta = jnp.pad(p_meta, ((LANES - N_META, 0), (0, 0)))
    kern = functools.partial(_pool_prompt_kernel, tiles_per_seq=seq // tm)
    return pl.pallas_call(
        kern,
        grid=(m // tm, ng),
        in_specs=[pl.BlockSpec((tm, gd), lambda i, g: (i, g)),
                  pl.BlockSpec((LANES, gd), lambda i, g: (jnp.maximum(i * hb - 1, 0), g)),
                  pl.BlockSpec((LANES, gd), lambda i, g: (0, g)),
                  pl.BlockSpec((None, tm, LANES + tm), lambda i, g: (g, 0, 0)),
                  pl.BlockSpec((None, gd, gd), lambda i, g: (g, 0, 0)),
                  pl.BlockSpec((1, gd), lambda i, g: (0, g))],
        out_specs=pl.BlockSpec((tm, gd), lambda i, g: (i, g)),
        out_shape=jax.ShapeDtypeStruct((m, pw), BF16),
        compiler_params=_params("parallel", "arbitrary"),
        name="pool_prompt",
    )(p, p, meta, _band_matrices(tm, LANES), w_pool, pool_scale)


_SAMPLE_EXT_ROWS = 32


def _pool_sample(state, p_new, w_pool, pool_scale, seqs_pref=8):
    db, ds, pw = p_new.shape
    ng, gd, _ = w_pool.shape
    ns = _tile(db, seqs_pref)
    assert POOL_BUF + ds <= _SAMPLE_EXT_ROWS and ds <= SUBLANES
    ext = jnp.concatenate(
        [state, p_new, jnp.zeros((db, _SAMPLE_EXT_ROWS - POOL_BUF - ds, pw), F32)], axis=1)
    ext = ext.reshape(db * _SAMPLE_EXT_ROWS, pw)
    one = np.zeros((len(POOL_WINDOWS), SUBLANES, _SAMPLE_EXT_ROWS), np.float32)
    for g, w in enumerate(POOL_WINDOWS):
        for r in range(ds):
            one[g, r, POOL_BUF + r - w + 1:POOL_BUF + r + 1] += 1.0 / w
            one[g, r, POOL_BUF + r] -= 1.0
    band = np.zeros((len(POOL_WINDOWS), ns * SUBLANES, ns * _SAMPLE_EXT_ROWS), np.float32)
    for s in range(ns):
        band[:, s * SUBLANES:(s + 1) * SUBLANES, s * _SAMPLE_EXT_ROWS:(s + 1) * _SAMPLE_EXT_ROWS] = one
    out = pl.pallas_call(
        _pool_sample_kernel,
        grid=(db // ns, ng),
        in_specs=[pl.BlockSpec((ns * _SAMPLE_EXT_ROWS, gd), lambda i, g: (i, g)),
                  pl.BlockSpec((None, ns * SUBLANES, ns * _SAMPLE_EXT_ROWS), lambda i, g: (g, 0, 0)),
                  pl.BlockSpec((None, gd, gd), lambda i, g: (g, 0, 0)),
                  pl.BlockSpec((1, gd), lambda i, g: (0, g))],
        out_specs=pl.BlockSpec((ns * SUBLANES, gd), lambda i, g: (i, g)),
        out_shape=jax.ShapeDtypeStruct((db * SUBLANES, pw), BF16),
        compiler_params=_params("parallel", "arbitrary"),
        name="pool_sample",
    )(ext, jnp.asarray(band, BF16), w_pool, pool_scale)
    return out.reshape(db, SUBLANES, pw)[:, :ds]


def _accumulate(o_ref, terms, col_chunk=1024):
    n = o_ref.shape[1]
    cn = min(n, col_chunk)
    for c in range(n // cn):
        cols = slice(c * cn, (c + 1) * cn)
        part = _dot(terms[0][0][...], terms[0][1][:, cols])
        for a_ref, w_ref in terms[1:]:
            part += _dot(a_ref[...], w_ref[:, cols])
        o_ref[:, cols] += part


def _outproj_kernel(a1_ref, a2_ref, w1_ref, w2_ref, x_ref, gpost_ref, gpre_ref, h_ref, hn_ref):
    kk = pl.program_id(1)

    @pl.when(kk == 0)
    def _():
        h_ref[...] = jnp.zeros_like(h_ref)

    _accumulate(h_ref, [(a1_ref, w1_ref), (a2_ref, w2_ref)])

    @pl.when(kk == pl.num_programs(1) - 1)
    def _():
        gpost = gpost_ref[...]
        gpre = gpre_ref[...]

        def finish(rows):
            h = x_ref[rows, :] + _rms(h_ref[rows, :], gpost)
            h_ref[rows, :] = h
            hn_ref[rows, :] = _rms(h, gpre).astype(BF16)

        _for_row_chunks(h_ref.shape[0], 16, finish)


def _out_projection(a1, a2, w_out, x, g_post, g_pre, tm_pref=512, tk_pref=256):
    m, k1 = a1.shape
    d = w_out.shape[1]
    tm = _tile(m, tm_pref)
    tk = _tile(k1, tk_pref)
    nk = k1 // tk
    row1 = pl.Buffered(1)
    return pl.pallas_call(
        _outproj_kernel,
        grid=(m // tm, nk),
        in_specs=[pl.BlockSpec((tm, tk), lambda i, k: (i, k)),
                  pl.BlockSpec((tm, tk), lambda i, k: (i, k)),
                  pl.BlockSpec((tk, d), lambda i, k: (k, 0)),
                  pl.BlockSpec((tk, d), lambda i, k: (k + nk, 0)),
                  pl.BlockSpec((tm, d), lambda i, k: (i, 0), pipeline_mode=row1),
                  pl.BlockSpec((1, d), lambda i, k: (0, 0)),
                  pl.BlockSpec((1, d), lambda i, k: (0, 0))],
        out_specs=[pl.BlockSpec((tm, d), lambda i, k: (i, 0)),
                   pl.BlockSpec((tm, d), lambda i, k: (i, 0))],
        out_shape=[jax.ShapeDtypeStruct((m, d), F32), jax.ShapeDtypeStruct((m, d), BF16)],
        compiler_params=_params("parallel", "arbitrary"),
        name="out_projection",
    )(a1, a2, w_out, w_out, x, g_post, g_pre)


def _up_kernel(hn_ref, w_ref, o_ref):
    u = jnp.maximum(_dot(hn_ref[...], w_ref[...]), 0.0)
    o_ref[...] = (u * u).astype(o_ref.dtype)


def _mlp_up(hn, w_up, tm_pref=1024, tn_pref=512):
    m, d = hn.shape
    f = w_up.shape[1]
    tm = _tile(m, tm_pref)
    tn = _tile(f, tn_pref)
    return pl.pallas_call(
        _up_kernel,
        grid=(m // tm, f // tn),
        in_specs=[pl.BlockSpec((tm, d), lambda i, j: (i, 0)),
                  pl.BlockSpec((d, tn), lambda i, j: (0, j))],
        out_specs=pl.BlockSpec((tm, tn), lambda i, j: (i, j)),
        out_shape=jax.ShapeDtypeStruct((m, f), BF16),
        compiler_params=_params("parallel", "arbitrary"),
        name="mlp_up",
    )(hn, w_up)


def _down_kernel(u_ref, w_ref, h_ref, g_ref, y_ref):
    kk = pl.program_id(1)

    @pl.when(kk == 0)
    def _():
        y_ref[...] = jnp.zeros_like(y_ref)

    _accumulate(y_ref, [(u_ref, w_ref)])

    @pl.when(kk == pl.num_programs(1) - 1)
    def _():
        g = g_ref[...]

        def finish(rows):
            y_ref[rows, :] = h_ref[rows, :] + _rms(y_ref[rows, :], g)

        _for_row_chunks(y_ref.shape[0], 16, finish)


def _mlp_down(u2, w_down, h, g_post, tm_pref=512, tk_pref=512):
    m, f = u2.shape
    d = w_down.shape[1]
    tm = _tile(m, tm_pref)
    tk = _tile(f, tk_pref)
    return pl.pallas_call(
        _down_kernel,
        grid=(m // tm, f // tk),
        in_specs=[pl.BlockSpec((tm, tk), lambda i, k: (i, k)),
                  pl.BlockSpec((tk, d), lambda i, k: (k, 0)),
                  pl.BlockSpec((tm, d), lambda i, k: (i, 0), pipeline_mode=pl.Buffered(1)),
                  pl.BlockSpec((1, d), lambda i, k: (0, 0))],
        out_specs=pl.BlockSpec((tm, d), lambda i, k: (i, 0)),
        out_shape=jax.ShapeDtypeStruct((m, d), F32),
        compiler_params=_params("parallel", "arbitrary"),
        name="mlp_down",
    )(u2, w_down, h, g_post)


def _finish_layer(x, attn, pool, w_out, g_mix_post, g_mlp_pre, w_up, w_down, g_mlp_post):
    h, hn = _out_projection(attn, pool, w_out, x, g_mix_post, g_mlp_pre)
    u2 = _mlp_up(hn, w_up)
    return _mlp_down(u2, w_down, h, g_mlp_post)


def kernel(x_prompt, x_sample, cache_k, cache_v, state_pool, page_table, meta, g_mix_pre, w_in, sb_bias,
           w_pool, pool_scale, w_out, g_mix_post, g_mlp_pre, w_up, w_down, g_mlp_post):
    depth = w_in.shape[0]
    assert depth == 1, "meta rows skip the MLP, which is only valid for a single layer"
    b, s, d = x_prompt.shape
    db, ds, _ = x_sample.shape
    nh = cache_k.shape[3]
    sbw = nh * HEAD_DIM
    pw = w_pool.shape[1] * w_pool.shape[2]
    layer = 0

    w_in_b = w_in[layer].astype(BF16)
    w_out_b = w_out[layer].astype(BF16)
    w_up_b = w_up[layer].astype(BF16)
    w_down_b = w_down[layer].astype(BF16)
    w_pool_b = w_pool[layer].astype(BF16)
    g_pre = g_mix_pre[layer][None, :]
    scale_row = pool_scale[layer][None, :]
    mlp = (w_out_b, g_mix_post[layer][None, :], g_mlp_pre[layer][None, :], w_up_b, w_down_b,
           g_mlp_post[layer][None, :])
    bias = sb_bias[layer].astype(F32)

    xp = x_prompt.reshape(b * s, d)
    q_p, k_p, v_p, p_p = _in_projection(xp, g_pre, w_in_b, sbw, pw)
    xs = x_sample.reshape(db * ds, d)
    n_s = db * ds
    q_sm, k_sm, v_sm, p_sm = _in_projection(jnp.concatenate([xs, meta.astype(F32)], axis=0), g_pre, w_in_b, sbw, pw,
                                            tm_pref=n_s + N_META)
    q_s, k_s, v_s, p_s = q_sm[:n_s], k_sm[:n_s], v_sm[:n_s], p_sm[:n_s]
    k_m, v_m, p_m = k_sm[n_s:], v_sm[n_s:], p_sm[n_s:]

    attn_p = _prompt_attention(q_p, k_p, v_p, k_m, v_m, bias, b, s)
    pool_p = _pool_prompt(p_p, p_m, w_pool_b, scale_row, s)
    y_p = _finish_layer(xp, attn_p, pool_p, *mlp)

    attn_s = _sample_attention(q_s.reshape(db, ds, sbw), k_s.reshape(db, ds, sbw), v_s.reshape(db, ds, sbw),
                               cache_k[layer], cache_v[layer], page_table, bias)
    pool_s = _pool_sample(state_pool[layer], p_s.reshape(db, ds, pw), w_pool_b, scale_row)
    y_s = _finish_layer(xs, attn_s.reshape(n_s, sbw), pool_s.reshape(n_s, pw), *mlp)

    def with_meta(a_meta, a):
        full = jnp.concatenate([jnp.broadcast_to(a_meta[None], (b, N_META, sbw)), a.reshape(b, s, sbw)], axis=1)
        return full.reshape(1, b, N_META + s, nh, HEAD_DIM)

    assert s >= POOL_BUF
    pool_prompt = p_p.reshape(b, s, pw)[:, -POOL_BUF:]
    pool_sample = jnp.concatenate([state_pool[layer], p_s.reshape(db, ds, pw)], axis=1)[:, -POOL_BUF:]
    return (y_p.reshape(b, s, d), y_s.reshape(db, ds, d),
            with_meta(k_m, k_p), with_meta(v_m, v_p), pool_prompt[None],
            k_s.reshape(1, db, ds, nh, HEAD_DIM), v_s.reshape(1, db, ds, nh, HEAD_DIM), pool_sample[None])
```

```python
import functools

import numpy as np
import jax
import jax.numpy as jnp
from jax import lax
from jax.experimental import pallas as pl
from jax.experimental.pallas import tpu as pltpu

F32 = jnp.float32
BF16 = jnp.bfloat16

HEAD_DIM = 128
N_META = 16
POOL_WINDOWS = (2, 4, 8, 16)
POOL_BUF = max(POOL_WINDOWS) - 1
EPS = 1e-6
LANES = 128
SUBLANES = 8
VMEM_LIMIT_BYTES = 56 * 1024 * 1024


def _tile(n, pref):
    if n <= pref:
        return n
    t = pref
    while n % t:
        t //= 2
    return t


def _params(*sem):
    return pltpu.CompilerParams(dimension_semantics=sem, vmem_limit_bytes=VMEM_LIMIT_BYTES)


def _dot(a, b):
    return jnp.dot(a, b, preferred_element_type=F32)


def _dot_nt(a, b):
    return lax.dot_general(a, b, (((1,), (1,)), ((), ())), preferred_element_type=F32)


def _split_bf16(x):
    hi = x.astype(BF16)
    lo = (x - hi.astype(F32)).astype(BF16)
    return hi, lo


def _rms(x, g):
    ms = jnp.mean(x * x, axis=-1, keepdims=True)
    return x * lax.rsqrt(ms + EPS) * g


def _for_row_chunks(n_rows, chunk, fn):
    chunk = _tile(n_rows, chunk)

    def body(c, carry):
        fn(pl.ds(pl.multiple_of(c * chunk, chunk), chunk))
        return carry

    lax.fori_loop(0, n_rows // chunk, body, 0)


def _inproj_kernel(x_ref, g_ref, wq_ref, wk_ref, wv_ref, wp_ref, q_ref, k_ref, v_ref, p_ref, xn_ref, *, q_scale):
    @pl.when(pl.program_id(1) == 0)
    def _():
        g = g_ref[...]

        def norm(rows):
            xn_ref[rows, :] = _rms(x_ref[rows, :], g).astype(BF16)

        _for_row_chunks(x_ref.shape[0], 64, norm)

    xn = xn_ref[...]
    q_ref[...] = (_dot(xn, wq_ref[...]) * q_scale).astype(q_ref.dtype)
    k_ref[...] = _dot(xn, wk_ref[...])
    v_ref[...] = _dot(xn, wv_ref[...])
    p_ref[...] = _dot(xn, wp_ref[...])


def _in_projection(x, g, w_in, sbw, pw, tm_pref=512, tn_pref=256):
    m, d = x.shape
    tm = _tile(m, tm_pref)
    tn = _tile(min(sbw, pw), tn_pref)
    assert sbw % tn == 0 and pw % tn == 0 and sbw == pw
    nj = sbw // tn

    def wspec(c):
        return pl.BlockSpec((d, tn), lambda i, j: (0, j + c * nj))

    ospec = pl.BlockSpec((tm, tn), lambda i, j: (i, j))
    return pl.pallas_call(
        functools.partial(_inproj_kernel, q_scale=HEAD_DIM ** -0.5),
        grid=(m // tm, nj),
        in_specs=[pl.BlockSpec((tm, d), lambda i, j: (i, 0)),
                  pl.BlockSpec((1, d), lambda i, j: (0, 0)),
                  wspec(0), wspec(1), wspec(2), wspec(3)],
        out_specs=[ospec, ospec, ospec, ospec],
        out_shape=[jax.ShapeDtypeStruct((m, sbw), BF16),
                   jax.ShapeDtypeStruct((m, sbw), F32),
                   jax.ShapeDtypeStruct((m, sbw), F32),
                   jax.ShapeDtypeStruct((m, pw), F32)],
        scratch_shapes=[pltpu.VMEM((tm, d), BF16)],
        compiler_params=_params("parallel", "arbitrary"),
        name="in_projection",
    )(x, g, w_in, w_in, w_in, w_in)


def _suffix_matrix(tk):
    j = np.arange(tk)[:, None]
    s = np.arange(tk)[None, :]
    return jnp.asarray((j > s).astype(np.float32), BF16)


def _sb_weights(z, valid, u, carry, stack=False):
    r, tk = z.shape
    sp = jnp.maximum(z, 0.0) + jnp.log(1.0 + jnp.exp(-jnp.abs(z)))
    if valid is not None:
        sp = jnp.where(valid, sp, 0.0)
    hi, lo = _split_bf16(sp)
    if stack:
        su = _dot(jnp.concatenate([hi, lo], axis=0), u)
        su = su[:r] + su[r:]
    else:
        su = _dot(hi, u) + _dot(lo, u)
    later = su + jnp.concatenate([carry] * (tk // LANES), axis=1)
    w = jnp.exp(z - sp - later)
    if valid is not None:
        w = jnp.where(valid, w, 0.0)
    total = jnp.broadcast_to(jnp.sum(sp, axis=1, keepdims=True), (r, LANES))
    return w, carry + total


def _prompt_attn_kernel(bias_ref, q_ref, k_ref, v_ref, km_ref, vm_ref, u1_ref, u1m_ref, o_ref,
                        kb_ref, vb_ref, carry_ref, acc_ref, *, tq, tk):
    h = pl.program_id(1)
    qi = pl.program_id(2)

    @pl.when(qi == 0)
    def _():
        kb_ref[...] = k_ref[...].astype(BF16)
        vb_ref[...] = v_ref[...].astype(BF16)

    bias = bias_ref[h]
    carry_ref[...] = jnp.zeros_like(carry_ref)
    acc_ref[...] = jnp.zeros_like(acc_ref)

    def block(r0, kblk, vblk, valid, u):
        z = _dot_nt(q_ref[r0:, :], kblk) + bias
        w, carry = _sb_weights(z, valid, u, carry_ref[r0:, :])
        carry_ref[r0:, :] = carry
        acc_ref[r0:, :] += _dot(w.astype(BF16), vblk)

    for d in reversed(range(tq // tk)):
        r0 = d * tk
        off = pl.multiple_of(qi * tq + r0, tk)
        row = lax.broadcasted_iota(jnp.int32, (tq - r0, tk), 0)
        col = lax.broadcasted_iota(jnp.int32, (tq - r0, tk), 1)
        block(r0, kb_ref[pl.ds(off, tk), :], vb_ref[pl.ds(off, tk), :], col < row, u1_ref[...])

    n_before = qi * (tq // tk)

    def body(it, c):
        off = pl.multiple_of((n_before - 1 - it) * tk, tk)
        block(0, kb_ref[pl.ds(off, tk), :], vb_ref[pl.ds(off, tk), :], None, u1_ref[...])
        return c

    lax.fori_loop(0, n_before, body, 0)

    colm = lax.broadcasted_iota(jnp.int32, (tq, LANES), 1)
    block(0, km_ref[...].astype(BF16), vm_ref[...].astype(BF16), colm < N_META, u1m_ref[...])
    o_ref[...] = acc_ref[...].astype(o_ref.dtype)


def _prompt_attention(q, k, v, k_meta, v_meta, bias, batch, seq, tq_pref=1024, tk_pref=256):
    m, sbw = q.shape
    nh = sbw // HEAD_DIM
    tq = _tile(seq, tq_pref)
    tk = _tile(tq, tk_pref)
    nq = seq // tq
    pad = ((0, LANES - N_META), (0, 0))
    km = jnp.pad(k_meta, pad)
    vm = jnp.pad(v_meta, pad)
    kern = functools.partial(_prompt_attn_kernel, tq=tq, tk=tk)
    qspec = pl.BlockSpec((tq, HEAD_DIM), lambda b, h, i: (b * nq + i, h))
    kvspec = pl.BlockSpec((seq, HEAD_DIM), lambda b, h, i: (b, h))
    mspec = pl.BlockSpec((LANES, HEAD_DIM), lambda b, h, i: (0, h))
    return pl.pallas_call(
        kern,
        grid=(batch, nh, nq),
        in_specs=[pl.BlockSpec(memory_space=pltpu.SMEM), qspec, kvspec, kvspec, mspec, mspec,
                  pl.BlockSpec((tk, tk), lambda b, h, i: (0, 0)),
                  pl.BlockSpec((LANES, LANES), lambda b, h, i: (0, 0))],
        out_specs=qspec,
        out_shape=jax.ShapeDtypeStruct((m, sbw), BF16),
        scratch_shapes=[pltpu.VMEM((seq, HEAD_DIM), BF16), pltpu.VMEM((seq, HEAD_DIM), BF16),
                        pltpu.VMEM((tq, LANES), F32), pltpu.VMEM((tq, HEAD_DIM), F32)],
        compiler_params=_params("parallel", "parallel", "arbitrary"),
        name="prompt_attention",
    )(bias, q, k, v, km, vm, _suffix_matrix(tk), _suffix_matrix(LANES))


def _sample_attn_kernel(pt_ref, q_ref, kn_ref, vn_ref, bias_ref, u1_ref, ug_ref, mask_ref, sel_ref, selt_ref, *rest,
                        nh, group):
    kc_refs = rest[:group]
    vc_refs = rest[group:2 * group]
    o_ref, carry_ref, acc_ref = rest[2 * group:]
    del pt_ref
    j = pl.program_id(1)
    q = q_ref[...]
    rows = q.shape[0]
    bias = bias_ref[...]

    def attend(kflats, vflats, valid, u):
        zs = []
        for kf in kflats:
            r = kf.shape[0]
            hi, lo = _split_bf16(_dot_nt(q, kf) * mask_ref[:, :r])
            zz = _dot(jnp.concatenate([hi, lo], axis=0), sel_ref[:r, :])
            zs.append(zz[:rows] + zz[rows:] + bias)
        w, carry = _sb_weights(jnp.concatenate(zs, axis=1), valid, u, carry_ref[...], stack=True)
        carry_ref[...] = carry
        w = w.astype(BF16)
        upd = acc_ref[...]
        for i, vf in enumerate(vflats):
            r = vf.shape[0]
            wide = _dot(w[:, i * LANES:(i + 1) * LANES], selt_ref[:, :r]) * mask_ref[:, :r]
            upd += _dot(wide.astype(BF16), vf)
        acc_ref[...] = upd

    @pl.when(j == 0)
    def _():
        carry_ref[...] = jnp.zeros_like(carry_ref)
        acc_ref[...] = jnp.zeros_like(acc_ref)
        t = lax.broadcasted_iota(jnp.int32, (rows, LANES), 0) // nh
        key = lax.broadcasted_iota(jnp.int32, (rows, LANES), 1)
        attend([kn_ref[...].astype(BF16)], [vn_ref[...].astype(BF16)], key < t, u1_ref[...])

    attend([kc_refs[c][...].astype(BF16) for c in reversed(range(group))],
           [vc_refs[c][...].astype(BF16) for c in reversed(range(group))], None, ug_ref[...])

    @pl.when(j == pl.num_programs(1) - 1)
    def _():
        o_ref[...] = acc_ref[...].astype(o_ref.dtype)


def _sample_attention(q, k_new, v_new, cache_k, cache_v, page_table, bias, group_pref=4):
    db, ds, sbw = q.shape
    nh = sbw // HEAD_DIM
    n_pages = page_table.shape[1]
    n_phys, page = cache_k.shape[:2]
    group = _tile(n_pages, group_pref)
    assert page == LANES and ds <= page
    rows = ds * nh
    flat = page * nh

    col_head = np.arange(flat) % nh
    mask = (np.arange(rows)[:, None] % nh == col_head[None, :]).astype(np.float32)
    sel = (np.arange(flat)[:, None] // nh == np.arange(LANES)[None, :]).astype(np.float32)
    bias_rows = jnp.broadcast_to(jnp.tile(bias.astype(F32), ds)[:, None], (rows, LANES))
    pt = page_table.reshape(-1).astype(jnp.int32)

    def const(shape):
        return pl.BlockSpec(shape, lambda n, j, pt: (0,) * len(shape))

    seq_spec = pl.BlockSpec((None, rows, HEAD_DIM), lambda n, j, pt: (n, 0, 0))

    def page_spec(c):
        def imap(n, j, pt):
            return (pt[n * n_pages + (n_pages - 1 - (j * group + c))], 0, 0)
        return pl.BlockSpec((None, flat, HEAD_DIM), imap)

    kern = functools.partial(_sample_attn_kernel, nh=nh, group=group)
    ck = cache_k.reshape(n_phys, flat, HEAD_DIM)
    cv = cache_v.reshape(n_phys, flat, HEAD_DIM)
    out = pl.pallas_call(
        kern,
        grid_spec=pltpu.PrefetchScalarGridSpec(
            num_scalar_prefetch=1,
            grid=(db, n_pages // group),
            in_specs=[seq_spec, seq_spec, seq_spec, const((rows, LANES)), const((page, page)),
                      const((group * page, group * page)),
                      const((rows, flat)), const((flat, LANES)), const((LANES, flat))]
                     + [page_spec(c) for c in range(group)] * 2,
            out_specs=seq_spec,
            scratch_shapes=[pltpu.VMEM((rows, LANES), F32), pltpu.VMEM((rows, HEAD_DIM), F32)]),
        out_shape=jax.ShapeDtypeStruct((db, rows, HEAD_DIM), BF16),
        compiler_params=_params("parallel", "arbitrary"),
        name="sample_attention",
    )(pt, q.reshape(db, rows, HEAD_DIM), k_new.reshape(db, rows, HEAD_DIM), v_new.reshape(db, rows, HEAD_DIM),
      bias_rows, _suffix_matrix(page), _suffix_matrix(group * page), jnp.asarray(mask), jnp.asarray(sel, BF16),
      jnp.asarray(sel.T, BF16),
      *([ck] * group), *([cv] * group))
    return out.reshape(db, ds, sbw)


def _pool_tail(bm, ext, w_ref, s_ref, o_ref):
    hi, lo = _split_bf16(ext)
    d = _dot(bm, hi) + _dot(bm, lo)
    o_ref[...] = (_dot(d.astype(BF16), w_ref[...]) * s_ref[...]).astype(o_ref.dtype)


def _pool_prompt_kernel(p_ref, halo_ref, meta_ref, b_ref, w_ref, s_ref, o_ref, *, tiles_per_seq):
    first = (pl.program_id(0) % tiles_per_seq) == 0
    halo = jnp.where(first, meta_ref[...], halo_ref[...])
    _pool_tail(b_ref[...], jnp.concatenate([halo, p_ref[...]], axis=0), w_ref, s_ref, o_ref)


def _pool_sample_kernel(e_ref, b_ref, w_ref, s_ref, o_ref):
    _pool_tail(b_ref[...], e_ref[...], w_ref, s_ref, o_ref)


def _band_matrices(n_out, n_halo):
    b = np.zeros((len(POOL_WINDOWS), n_out, n_halo + n_out), np.float32)
    for g, w in enumerate(POOL_WINDOWS):
        for r in range(n_out):
            b[g, r, n_halo + r - w + 1:n_halo + r + 1] += 1.0 / w
            b[g, r, n_halo + r] -= 1.0
    return jnp.asarray(b, BF16)


def _pool_prompt(p, p_meta, w_pool, pool_scale, seq, tm_pref=256):
    m, pw = p.shape
    ng, gd, _ = w_pool.shape
    tm = _tile(seq, tm_pref)
    assert tm % LANES == 0
    hb = tm // LANES
    meta = jnp.pad(p_meta, ((LANES - N_META, 0), (0, 0)))
    kern = functools.partial(_pool_prompt_kernel, tiles_per_seq=seq // tm)
    return pl.pallas_call(
        kern,
        grid=(m // tm, ng),
        in_specs=[pl.BlockSpec((tm, gd), lambda i, g: (i, g)),
                  pl.BlockSpec((LANES, gd), lambda i, g: (jnp.maximum(i * hb - 1, 0), g)),
                  pl.BlockSpec((LANES, gd), lambda i, g: (0, g)),
                  pl.BlockSpec((None, tm, LANES + tm), lambda i, g: (g, 0, 0)),
                  pl.BlockSpec((None, gd, gd), lambda i, g: (g, 0, 0)),
                  pl.BlockSpec((1, gd), lambda i, g: (0, g))],
        out_specs=pl.BlockSpec((tm, gd), lambda i, g: (i, g)),
        out_shape=jax.ShapeDtypeStruct((m, pw), BF16),
        compiler_params=_params("parallel", "arbitrary"),
        name="pool_prompt",
    )(p, p, meta, _band_matrices(tm, LANES), w_pool, pool_scale)


_SAMPLE_EXT_ROWS = 32


def _pool_sample(state, p_new, w_pool, pool_scale, seqs_pref=8):
    db, ds, pw = p_new.shape
    ng, gd, _ = w_pool.shape
    ns = _tile(db, seqs_pref)
    assert POOL_BUF + ds <= _SAMPLE_EXT_ROWS and ds <= SUBLANES
    ext = jnp.concatenate(
        [state, p_new, jnp.zeros((db, _SAMPLE_EXT_ROWS - POOL_BUF - ds, pw), F32)], axis=1)
    ext = ext.reshape(db * _SAMPLE_EXT_ROWS, pw)
    one = np.zeros((len(POOL_WINDOWS), SUBLANES, _SAMPLE_EXT_ROWS), np.float32)
    for g, w in enumerate(POOL_WINDOWS):
        for r in range(ds):
            one[g, r, POOL_BUF + r - w + 1:POOL_BUF + r + 1] += 1.0 / w
            one[g, r, POOL_BUF + r] -= 1.0
    band = np.zeros((len(POOL_WINDOWS), ns * SUBLANES, ns * _SAMPLE_EXT_ROWS), np.float32)
    for s in range(ns):
        band[:, s * SUBLANES:(s + 1) * SUBLANES, s * _SAMPLE_EXT_ROWS:(s + 1) * _SAMPLE_EXT_ROWS] = one
    out = pl.pallas_call(
        _pool_sample_kernel,
        grid=(db // ns, ng),
        in_specs=[pl.BlockSpec((ns * _SAMPLE_EXT_ROWS, gd), lambda i, g: (i, g)),
                  pl.BlockSpec((None, ns * SUBLANES, ns * _SAMPLE_EXT_ROWS), lambda i, g: (g, 0, 0)),
                  pl.BlockSpec((None, gd, gd), lambda i, g: (g, 0, 0)),
                  pl.BlockSpec((1, gd), lambda i, g: (0, g))],
        out_specs=pl.BlockSpec((ns * SUBLANES, gd), lambda i, g: (i, g)),
        out_shape=jax.ShapeDtypeStruct((db * SUBLANES, pw), BF16),
        compiler_params=_params("parallel", "arbitrary"),
        name="pool_sample",
    )(ext, jnp.asarray(band, BF16), w_pool, pool_scale)
    return out.reshape(db, SUBLANES, pw)[:, :ds]


def _accumulate(o_ref, terms, col_chunk=1024):
    n = o_ref.shape[1]
    cn = min(n, col_chunk)
    for c in range(n // cn):
        cols = slice(c * cn, (c + 1) * cn)
        part = _dot(terms[0][0][...], terms[0][1][:, cols])
        for a_ref, w_ref in terms[1:]:
            part += _dot(a_ref[...], w_ref[:, cols])
        o_ref[:, cols] += part


def _outproj_kernel(a1_ref, a2_ref, w1_ref, w2_ref, x_ref, gpost_ref, gpre_ref, h_ref, hn_ref):
    kk = pl.program_id(1)

    @pl.when(kk == 0)
    def _():
        h_ref[...] = jnp.zeros_like(h_ref)

    _accumulate(h_ref, [(a1_ref, w1_ref), (a2_ref, w2_ref)])

    @pl.when(kk == pl.num_programs(1) - 1)
    def _():
        gpost = gpost_ref[...]
        gpre = gpre_ref[...]

        def finish(rows):
            h = x_ref[rows, :] + _rms(h_ref[rows, :], gpost)
            h_ref[rows, :] = h
            hn_ref[rows, :] = _rms(h, gpre).astype(BF16)

        _for_row_chunks(h_ref.shape[0], 32, finish)


def _out_projection(a1, a2, w_out, x, g_post, g_pre, tm_pref=512, tk_pref=512):
    m, k1 = a1.shape
    d = w_out.shape[1]
    tm = _tile(m, tm_pref)
    tk = _tile(k1, tk_pref)
    nk = k1 // tk
    return pl.pallas_call(
        _outproj_kernel,
        grid=(m // tm, nk),
        in_specs=[pl.BlockSpec((tm, tk), lambda i, k: (i, k)),
                  pl.BlockSpec((tm, tk), lambda i, k: (i, k)),
                  pl.BlockSpec((tk, d), lambda i, k: (k, 0)),
                  pl.BlockSpec((tk, d), lambda i, k: (k + nk, 0)),
                  pl.BlockSpec((tm, d), lambda i, k: (i, 0), pipeline_mode=pl.Buffered(1)),
                  pl.BlockSpec((1, d), lambda i, k: (0, 0)),
                  pl.BlockSpec((1, d), lambda i, k: (0, 0))],
        out_specs=[pl.BlockSpec((tm, d), lambda i, k: (i, 0)),
                   pl.BlockSpec((tm, d), lambda i, k: (i, 0))],
        out_shape=[jax.ShapeDtypeStruct((m, d), F32), jax.ShapeDtypeStruct((m, d), BF16)],
        compiler_params=_params("parallel", "arbitrary"),
        name="out_projection",
    )(a1, a2, w_out, w_out, x, g_post, g_pre)


def _up_kernel(hn_ref, w_ref, o_ref, wb_ref):
    @pl.when(pl.program_id(1) == 0)
    def _():
        def cast(rows):
            wb_ref[rows, :] = w_ref[rows, :].astype(BF16)

        _for_row_chunks(w_ref.shape[0], 256, cast)

    u = jnp.maximum(_dot(hn_ref[...], wb_ref[...]), 0.0)
    o_ref[...] = (u * u).astype(o_ref.dtype)


def _mlp_up(hn, w_up, tm_pref=1024, tn_pref=512):
    m, d = hn.shape
    f = w_up.shape[1]
    tm = _tile(m, tm_pref)
    tn = _tile(f, tn_pref)
    return pl.pallas_call(
        _up_kernel,
        grid=(f // tn, m // tm),
        in_specs=[pl.BlockSpec((tm, d), lambda j, i: (i, 0)),
                  pl.BlockSpec((d, tn), lambda j, i: (0, j))],
        out_specs=pl.BlockSpec((tm, tn), lambda j, i: (i, j)),
        out_shape=jax.ShapeDtypeStruct((m, f), BF16),
        scratch_shapes=[pltpu.VMEM((d, tn), BF16)],
        compiler_params=_params("parallel", "arbitrary"),
        name="mlp_up",
    )(hn, w_up)


def _down_kernel(u_ref, w_ref, h_ref, g_ref, y_ref):
    kk = pl.program_id(1)

    @pl.when(kk == 0)
    def _():
        y_ref[...] = jnp.zeros_like(y_ref)

    _accumulate(y_ref, [(u_ref, w_ref)])

    @pl.when(kk == pl.num_programs(1) - 1)
    def _():
        g = g_ref[...]

        def finish(rows):
            y_ref[rows, :] = h_ref[rows, :] + _rms(y_ref[rows, :], g)

        _for_row_chunks(y_ref.shape[0], 32, finish)


def _mlp_down(u2, w_down, h, g_post, tm_pref=512, tk_pref=1024):
    m, f = u2.shape
    d = w_down.shape[1]
    tm = _tile(m, tm_pref)
    tk = _tile(f, tk_pref)
    return pl.pallas_call(
        _down_kernel,
        grid=(m // tm, f // tk),
        in_specs=[pl.BlockSpec((tm, tk), lambda i, k: (i, k)),
                  pl.BlockSpec((tk, d), lambda i, k: (k, 0)),
                  pl.BlockSpec((tm, d), lambda i, k: (i, 0), pipeline_mode=pl.Buffered(1)),
                  pl.BlockSpec((1, d), lambda i, k: (0, 0))],
        out_specs=pl.BlockSpec((tm, d), lambda i, k: (i, 0)),
        out_shape=jax.ShapeDtypeStruct((m, d), F32),
        compiler_params=_params("parallel", "arbitrary"),
        name="mlp_down",
    )(u2, w_down, h, g_post)


def _finish_layer(x, attn, pool, w_out, g_mix_post, g_mlp_pre, w_up, w_down, g_mlp_post):
    h, hn = _out_projection(attn, pool, w_out, x, g_mix_post, g_mlp_pre)
    u2 = _mlp_up(hn, w_up)
    return _mlp_down(u2, w_down, h, g_mlp_post)


def kernel(x_prompt, x_sample, cache_k, cache_v, state_pool, page_table, meta, g_mix_pre, w_in, sb_bias,
           w_pool, pool_scale, w_out, g_mix_post, g_mlp_pre, w_up, w_down, g_mlp_post):
    depth = w_in.shape[0]
    assert depth == 1, "meta rows skip the MLP, which is only valid for a single layer"
    b, s, d = x_prompt.shape
    db, ds, _ = x_sample.shape
    nh = cache_k.shape[3]
    sbw = nh * HEAD_DIM
    pw = w_pool.shape[1] * w_pool.shape[2]
    layer = 0

    w_in_b = w_in[layer].astype(BF16)
    w_out_b = w_out[layer].astype(BF16)
    w_down_b = w_down[layer].astype(BF16)
    w_pool_b = w_pool[layer].astype(BF16)
    g_pre = g_mix_pre[layer][None, :]
    scale_row = pool_scale[layer][None, :]
    mlp = (w_out_b, g_mix_post[layer][None, :], g_mlp_pre[layer][None, :], w_up[layer], w_down_b,
           g_mlp_post[layer][None, :])
    bias = sb_bias[layer].astype(F32)

    xp = x_prompt.reshape(b * s, d)
    q_p, k_p, v_p, p_p = _in_projection(xp, g_pre, w_in_b, sbw, pw)
    xs = x_sample.reshape(db * ds, d)
    n_s = db * ds
    q_sm, k_sm, v_sm, p_sm = _in_projection(jnp.concatenate([xs, meta.astype(F32)], axis=0), g_pre, w_in_b, sbw, pw,
                                            tm_pref=n_s + N_META)
    q_s, k_s, v_s, p_s = q_sm[:n_s], k_sm[:n_s], v_sm[:n_s], p_sm[:n_s]
    k_m, v_m, p_m = k_sm[n_s:], v_sm[n_s:], p_sm[n_s:]

    attn_p = _prompt_attention(q_p, k_p, v_p, k_m, v_m, bias, b, s)
    pool_p = _pool_prompt(p_p, p_m, w_pool_b, scale_row, s)
    y_p = _finish_layer(xp, attn_p, pool_p, *mlp)

    attn_s = _sample_attention(q_s.reshape(db, ds, sbw), k_s.reshape(db, ds, sbw), v_s.reshape(db, ds, sbw),
                               cache_k[layer], cache_v[layer], page_table, bias)
    pool_s = _pool_sample(state_pool[layer], p_s.reshape(db, ds, pw), w_pool_b, scale_row)
    y_s = _finish_layer(xs, attn_s.reshape(n_s, sbw), pool_s.reshape(n_s, pw), *mlp)

    def with_meta(a_meta, a):
        full = jnp.concatenate([jnp.broadcast_to(a_meta[None], (b, N_META, sbw)), a.reshape(b, s, sbw)], axis=1)
        return full.reshape(1, b, N_META + s, nh, HEAD_DIM)

    assert s >= POOL_BUF
    pool_prompt = p_p.reshape(b, s, pw)[:, -POOL_BUF:]
    pool_sample = jnp.concatenate([state_pool[layer], p_s.reshape(db, ds, pw)], axis=1)[:, -POOL_BUF:]
    return (y_p.reshape(b, s, d), y_s.reshape(db, ds, d),
            with_meta(k_m, k_p), with_meta(v_m, v_p), pool_prompt[None],
            k_s.reshape(1, db, ds, nh, HEAD_DIM), v_s.reshape(1, db, ds, nh, HEAD_DIM), pool_sample[None])
```

```python
import functools

import numpy as np
import jax
import jax.numpy as jnp
from jax import lax
from jax.experimental import pallas as pl
from jax.experimental.pallas import tpu as pltpu

F32 = jnp.float32
BF16 = jnp.bfloat16

HEAD_DIM = 128
N_META = 16
POOL_WINDOWS = (2, 4, 8, 16)
POOL_BUF = max(POOL_WINDOWS) - 1
EPS = 1e-6
LOG2E = 1.4426950408889634
LANES = 128
SUBLANES = 8
VMEM_LIMIT_BYTES = 56 * 1024 * 1024


def _tile(n, pref):
    if n <= pref:
        return n
    t = pref
    while n % t:
        t //= 2
    return t


def _params(*sem):
    return pltpu.CompilerParams(dimension_semantics=sem, vmem_limit_bytes=VMEM_LIMIT_BYTES)


def _dot(a, b):
    return jnp.dot(a, b, preferred_element_type=F32)


def _dot_nt(a, b):
    return lax.dot_general(a, b, (((1,), (1,)), ((), ())), preferred_element_type=F32)


def _split_bf16(x):
    hi = x.astype(BF16)
    lo = (x - hi.astype(F32)).astype(BF16)
    return hi, lo


def _rms(x, g):
    ms = jnp.mean(x * x, axis=-1, keepdims=True)
    return x * lax.rsqrt(ms + EPS) * g


def _for_row_chunks(n_rows, chunk, fn):
    chunk = _tile(n_rows, chunk)

    def body(c, carry):
        fn(pl.ds(pl.multiple_of(c * chunk, chunk), chunk))
        return carry

    lax.fori_loop(0, n_rows // chunk, body, 0)


def _inproj_kernel(x_ref, g_ref, wq_ref, wk_ref, wv_ref, wp_ref, q_ref, k_ref, v_ref, p_ref, xn_ref, *, q_scale):
    @pl.when(pl.program_id(1) == 0)
    def _():
        g = g_ref[...]

        def norm(rows):
            xn_ref[rows, :] = _rms(x_ref[rows, :], g).astype(BF16)

        _for_row_chunks(x_ref.shape[0], 64, norm)

    xn = xn_ref[...]
    q_ref[...] = (_dot(xn, wq_ref[...]) * q_scale).astype(q_ref.dtype)
    k_ref[...] = _dot(xn, wk_ref[...])
    v_ref[...] = _dot(xn, wv_ref[...])
    p_ref[...] = _dot(xn, wp_ref[...])


def _in_projection(x, g, w_in, sbw, pw, tm_pref=512, tn_pref=256):
    m, d = x.shape
    tm = _tile(m, tm_pref)
    tn = _tile(min(sbw, pw), tn_pref)
    assert sbw % tn == 0 and pw % tn == 0 and sbw == pw
    nj = sbw // tn

    def wspec(c):
        return pl.BlockSpec((d, tn), lambda i, j: (0, j + c * nj))

    ospec = pl.BlockSpec((tm, tn), lambda i, j: (i, j))
    return pl.pallas_call(
        functools.partial(_inproj_kernel, q_scale=HEAD_DIM ** -0.5 * LOG2E),
        grid=(m // tm, nj),
        in_specs=[pl.BlockSpec((tm, d), lambda i, j: (i, 0)),
                  pl.BlockSpec((1, d), lambda i, j: (0, 0)),
                  wspec(0), wspec(1), wspec(2), wspec(3)],
        out_specs=[ospec, ospec, ospec, ospec],
        out_shape=[jax.ShapeDtypeStruct((m, sbw), BF16),
                   jax.ShapeDtypeStruct((m, sbw), F32),
                   jax.ShapeDtypeStruct((m, sbw), F32),
                   jax.ShapeDtypeStruct((m, pw), F32)],
        scratch_shapes=[pltpu.VMEM((tm, d), BF16)],
        compiler_params=_params("parallel", "arbitrary"),
        name="in_projection",
    )(x, g, w_in, w_in, w_in, w_in)


def _suffix_matrix(tk, copies=1):
    j = np.arange(tk)[:, None]
    s = np.arange(tk)[None, :]
    return jnp.asarray(np.tile((j > s).astype(np.float32), (copies, 1)), BF16)


def _sb_weights(z, valid, u, carry, stack=False):
    r, tk = z.shape
    sp = jnp.maximum(z, 0.0) + jnp.log2(1.0 + jnp.exp2(-jnp.abs(z)))
    if valid is not None:
        sp = jnp.where(valid, sp, 0.0)
    hi, lo = _split_bf16(sp)
    if stack:
        su = _dot(jnp.concatenate([hi, lo], axis=0), u)
        su = su[:r] + su[r:]
    else:
        su = _dot(jnp.concatenate([hi, lo], axis=1), u)
    later = su + jnp.concatenate([carry] * (tk // LANES), axis=1)
    w = jnp.exp2(z - sp - later)
    if valid is not None:
        w = jnp.where(valid, w, 0.0)
    total = jnp.broadcast_to(jnp.sum(sp, axis=1, keepdims=True), (r, LANES))
    return w, carry + total


def _prompt_attn_kernel(bias_ref, q_ref, k_ref, v_ref, km_ref, vm_ref, u1_ref, u1m_ref, o_ref,
                        kb_ref, vb_ref, carry_ref, acc_ref, *, tq, tk):
    h = pl.program_id(1)
    qi = pl.program_id(2)

    @pl.when(qi == 0)
    def _():
        kb_ref[...] = k_ref[...].astype(BF16)
        vb_ref[...] = v_ref[...].astype(BF16)

    bias = bias_ref[h]
    carry_ref[...] = jnp.zeros_like(carry_ref)
    acc_ref[...] = jnp.zeros_like(acc_ref)

    def block(r0, kblk, vblk, valid, u):
        z = _dot_nt(q_ref[r0:, :], kblk) + bias
        w, carry = _sb_weights(z, valid, u, carry_ref[r0:, :])
        carry_ref[r0:, :] = carry
        acc_ref[r0:, :] += _dot(w.astype(BF16), vblk)

    for d in reversed(range(tq // tk)):
        r0 = d * tk
        off = pl.multiple_of(qi * tq + r0, tk)
        row = lax.broadcasted_iota(jnp.int32, (tq - r0, tk), 0)
        col = lax.broadcasted_iota(jnp.int32, (tq - r0, tk), 1)
        block(r0, kb_ref[pl.ds(off, tk), :], vb_ref[pl.ds(off, tk), :], col < row, u1_ref[...])

    n_before = qi * (tq // tk)

    def body(it, c):
        off = pl.multiple_of((n_before - 1 - it) * tk, tk)
        block(0, kb_ref[pl.ds(off, tk), :], vb_ref[pl.ds(off, tk), :], None, u1_ref[...])
        return c

    lax.fori_loop(0, n_before, body, 0)

    colm = lax.broadcasted_iota(jnp.int32, (tq, LANES), 1)
    block(0, km_ref[...].astype(BF16), vm_ref[...].astype(BF16), colm < N_META, u1m_ref[...])
    o_ref[...] = acc_ref[...].astype(o_ref.dtype)


def _prompt_attention(q, k, v, k_meta, v_meta, bias, batch, seq, tq_pref=2048, tk_pref=256):
    m, sbw = q.shape
    nh = sbw // HEAD_DIM
    tq = _tile(seq, tq_pref)
    tk = _tile(tq, tk_pref)
    nq = seq // tq
    pad = ((0, LANES - N_META), (0, 0))
    km = jnp.pad(k_meta, pad)
    vm = jnp.pad(v_meta, pad)
    kern = functools.partial(_prompt_attn_kernel, tq=tq, tk=tk)
    qspec = pl.BlockSpec((tq, HEAD_DIM), lambda b, h, i: (b * nq + i, h))
    kvspec = pl.BlockSpec((seq, HEAD_DIM), lambda b, h, i: (b, h))
    mspec = pl.BlockSpec((LANES, HEAD_DIM), lambda b, h, i: (0, h))
    return pl.pallas_call(
        kern,
        grid=(batch, nh, nq),
        in_specs=[pl.BlockSpec(memory_space=pltpu.SMEM), qspec, kvspec, kvspec, mspec, mspec,
                  pl.BlockSpec((2 * tk, tk), lambda b, h, i: (0, 0)),
                  pl.BlockSpec((2 * LANES, LANES), lambda b, h, i: (0, 0))],
        out_specs=qspec,
        out_shape=jax.ShapeDtypeStruct((m, sbw), BF16),
        scratch_shapes=[pltpu.VMEM((seq, HEAD_DIM), BF16), pltpu.VMEM((seq, HEAD_DIM), BF16),
                        pltpu.VMEM((tq, LANES), F32), pltpu.VMEM((tq, HEAD_DIM), F32)],
        compiler_params=_params("parallel", "parallel", "arbitrary"),
        name="prompt_attention",
    )(bias, q, k, v, km, vm, _suffix_matrix(tk, 2), _suffix_matrix(LANES, 2))


HEAD_TILE = SUBLANES
PAGES_PER_BLOCK = 4


def _sample_attn_kernel(pt_ref, q_ref, kn_ref, vn_ref, bias_ref, u1_ref, ug_ref, mask_ref, sel_ref, selt_ref, *rest,
                        group):
    kc_refs = rest[:group]
    vc_refs = rest[group:2 * group]
    o_ref, carry_ref, acc_ref = rest[2 * group:]
    del pt_ref
    j = pl.program_id(1)
    n_half, hr = q_ref.shape[:2]
    bias = bias_ref[...]

    def attend(k_of, v_of, n_blocks, valid, u):
        r = k_of(0, 0).shape[0]
        m = mask_ref[:, :r]
        halves = []
        for half in range(n_half):
            kcat = jnp.concatenate([k_of(i, half) for i in range(n_blocks)], axis=0)
            zt = _dot_nt(q_ref[half], kcat)
            for i in range(n_blocks):
                halves.extend(_split_bf16(zt[:, i * r:(i + 1) * r] * m))
        zz = _dot(jnp.concatenate(halves, axis=0), sel_ref[:r, :])
        zs = []
        for i in range(n_blocks):
            parts = []
            for half in range(n_half):
                base = (half * n_blocks + i) * 2 * hr
                parts.append(zz[base:base + hr] + zz[base + hr:base + 2 * hr])
            zs.append(jnp.concatenate(parts, axis=0) + bias)
        w, carry = _sb_weights(jnp.concatenate(zs, axis=1), valid, u, carry_ref[...], stack=True)
        carry_ref[...] = carry
        w = w.astype(BF16)
        wrows = jnp.concatenate([w[half * hr:(half + 1) * hr, i * LANES:(i + 1) * LANES]
                                 for half in range(n_half) for i in range(n_blocks)], axis=0)
        wide = _dot(wrows, selt_ref[:, :r])
        for half in range(n_half):
            lhs = jnp.concatenate(
                [(wide[(half * n_blocks + i) * hr:(half * n_blocks + i + 1) * hr] * m).astype(BF16)
                 for i in range(n_blocks)], axis=1)
            vcat = jnp.concatenate([v_of(i, half) for i in range(n_blocks)], axis=0)
            acc_ref[half] += _dot(lhs, vcat)

    @pl.when(j == 0)
    def _():
        carry_ref[...] = jnp.zeros_like(carry_ref)
        acc_ref[...] = jnp.zeros_like(acc_ref)
        t = (lax.broadcasted_iota(jnp.int32, (n_half * hr, LANES), 0) % hr) // HEAD_TILE
        key = lax.broadcasted_iota(jnp.int32, (n_half * hr, LANES), 1)
        attend(lambda i, half: kn_ref[half].astype(BF16), lambda i, half: vn_ref[half].astype(BF16), 1,
               key < t, u1_ref[...])

    def half_page(ref, half):
        rows = ref.shape[0] // n_half
        x = ref[pl.ds(half, rows, stride=n_half), :, :]
        return x.reshape(rows * HEAD_TILE, HEAD_DIM).astype(BF16)

    for first in range(0, group, PAGES_PER_BLOCK):
        last = first + PAGES_PER_BLOCK - 1
        attend(lambda i, half: half_page(kc_refs[last - i], half),
               lambda i, half: half_page(vc_refs[last - i], half), PAGES_PER_BLOCK, None, ug_ref[...])

    @pl.when(j == pl.num_programs(1) - 1)
    def _():
        o_ref[...] = acc_ref[...].astype(o_ref.dtype)


def _sample_attention(q, k_new, v_new, cache_k, cache_v, page_table, bias, group_pref=8):
    db, ds, sbw = q.shape
    nh = sbw // HEAD_DIM
    n_pages = page_table.shape[1]
    n_phys, page = cache_k.shape[:2]
    group = _tile(n_pages, group_pref)
    assert page == LANES and ds <= page and nh % HEAD_TILE == 0 and group % PAGES_PER_BLOCK == 0
    n_half = nh // HEAD_TILE
    hr = ds * HEAD_TILE
    flat = page * HEAD_TILE

    def by_half(a):
        a = a.reshape(db, ds, n_half, HEAD_TILE, HEAD_DIM).transpose(0, 2, 1, 3, 4)
        return a.reshape(db, n_half, hr, HEAD_DIM)

    mask = (np.arange(hr)[:, None] % HEAD_TILE == np.arange(flat)[None, :] % HEAD_TILE).astype(np.float32)
    sel = (np.arange(flat)[:, None] // HEAD_TILE == np.arange(LANES)[None, :]).astype(np.float32)
    bias_rows = jnp.broadcast_to(bias.astype(F32).reshape(n_half, 1, HEAD_TILE, 1),
                                 (n_half, ds, HEAD_TILE, LANES)).reshape(n_half * hr, LANES)
    pt = page_table.reshape(-1).astype(jnp.int32)

    def const(shape):
        return pl.BlockSpec(shape, lambda n, j, pt: (0,) * len(shape))

    seq_spec = pl.BlockSpec((None, n_half, hr, HEAD_DIM), lambda n, j, pt: (n, 0, 0, 0))

    def page_spec(c):
        def imap(n, j, pt):
            return (pt[n * n_pages + (n_pages - 1 - (j * group + c))], 0, 0, 0)
        return pl.BlockSpec((None, page * n_half, HEAD_TILE, HEAD_DIM), imap)

    kern = functools.partial(_sample_attn_kernel, group=group)
    ck = cache_k.reshape(n_phys, page * n_half, HEAD_TILE, HEAD_DIM)
    cv = cache_v.reshape(n_phys, page * n_half, HEAD_TILE, HEAD_DIM)
    out = pl.pallas_call(
        kern,
        grid_spec=pltpu.PrefetchScalarGridSpec(
            num_scalar_prefetch=1,
            grid=(db, n_pages // group),
            in_specs=[seq_spec, seq_spec, seq_spec, const((n_half * hr, LANES)), const((page, page)),
                      const((PAGES_PER_BLOCK * page, PAGES_PER_BLOCK * page)),
                      const((hr, flat)), const((flat, LANES)), const((LANES, flat))]
                     + [page_spec(c) for c in range(group)] * 2,
            out_specs=seq_spec,
            scratch_shapes=[pltpu.VMEM((n_half * hr, LANES), F32), pltpu.VMEM((n_half, hr, HEAD_DIM), F32)]),
        out_shape=jax.ShapeDtypeStruct((db, n_half, hr, HEAD_DIM), BF16),
        compiler_params=_params("parallel", "arbitrary"),
        name="sample_attention",
    )(pt, by_half(q), by_half(k_new), by_half(v_new),
      bias_rows, _suffix_matrix(page), _suffix_matrix(PAGES_PER_BLOCK * page), jnp.asarray(mask), jnp.asarray(sel, BF16),
      jnp.asarray(sel.T, BF16), *([ck] * group), *([cv] * group))
    out = out.reshape(db, n_half, ds, HEAD_TILE, HEAD_DIM).transpose(0, 2, 1, 3, 4)
    return out.reshape(db, ds, sbw)


def _pool_tail(b_ref, ext, w_ref, s_ref, o_ref):
    ng, gd = w_ref.shape[:2]
    for g in range(ng):
        cols = slice(g * gd, (g + 1) * gd)
        hi, lo = _split_bf16(ext[:, cols])
        d = _dot(b_ref[g], jnp.concatenate([hi, lo], axis=0))
        o_ref[:, cols] = (_dot(d.astype(BF16), w_ref[g]) * s_ref[:, cols]).astype(o_ref.dtype)


def _pool_prompt_kernel(p_ref, halo_ref, meta_ref, b_ref, w_ref, s_ref, o_ref, *, tiles_per_seq):
    first = (pl.program_id(0) % tiles_per_seq) == 0
    halo = jnp.where(first, meta_ref[...], halo_ref[...])
    _pool_tail(b_ref, jnp.concatenate([halo, p_ref[...]], axis=0), w_ref, s_ref, o_ref)


def _pool_sample_kernel(e_ref, b_ref, w_ref, s_ref, o_ref):
    _pool_tail(b_ref, e_ref[...], w_ref, s_ref, o_ref)


def _window_rows(n_out, n_in, first):
    b = np.zeros((len(POOL_WINDOWS), n_out, n_in), np.float32)
    for g, w in enumerate(POOL_WINDOWS):
        for r in range(n_out):
            b[g, r, first + r - w + 1:first + r + 1] += 1.0 / w
            b[g, r, first + r] -= 1.0
    return b


def _pool_prompt(p, p_meta, w_pool, pool_scale, seq, tm_pref=512):
    m, pw = p.shape
    ng, gd, _ = w_pool.shape
    tm = _tile(seq, tm_pref)
    assert tm % LANES == 0
    hb = tm // LANES
    meta = jnp.pad(p_meta, ((LANES - N_META, 0), (0, 0)))
    band = np.tile(_window_rows(tm, LANES + tm, LANES), (1, 1, 2))
    kern = functools.partial(_pool_prompt_kernel, tiles_per_seq=seq // tm)
    once = pl.Buffered(1)
    return pl.pallas_call(
        kern,
        grid=(m // tm,),
        in_specs=[pl.BlockSpec((tm, pw), lambda i: (i, 0)),
                  pl.BlockSpec((LANES, pw), lambda i: (jnp.maximum(i * hb - 1, 0), 0)),
                  pl.BlockSpec((LANES, pw), lambda i: (0, 0), pipeline_mode=once),
                  pl.BlockSpec(band.shape, lambda i: (0, 0, 0), pipeline_mode=once),
                  pl.BlockSpec((ng, gd, gd), lambda i: (0, 0, 0), pipeline_mode=once),
                  pl.BlockSpec((1, pw), lambda i: (0, 0))],
        out_specs=pl.BlockSpec((tm, pw), lambda i: (i, 0)),
        out_shape=jax.ShapeDtypeStruct((m, pw), BF16),
        compiler_params=_params("parallel"),
        name="pool_prompt",
    )(p, p, meta, jnp.asarray(band, BF16), w_pool, pool_scale)


_SAMPLE_EXT_ROWS = 32


def _pool_sample(state, p_new, w_pool, pool_scale, seqs_pref=16):
    db, ds, pw = p_new.shape
    ng, gd, _ = w_pool.shape
    ns = _tile(db, seqs_pref)
    assert POOL_BUF + ds <= _SAMPLE_EXT_ROWS and ds <= SUBLANES
    ext = jnp.concatenate(
        [state, p_new, jnp.zeros((db, _SAMPLE_EXT_ROWS - POOL_BUF - ds, pw), F32)], axis=1)
    ext = ext.reshape(db * _SAMPLE_EXT_ROWS, pw)
    one = np.zeros((len(POOL_WINDOWS), SUBLANES, _SAMPLE_EXT_ROWS), np.float32)
    one[:, :ds] = _window_rows(ds, _SAMPLE_EXT_ROWS, POOL_BUF)
    band = np.zeros((len(POOL_WINDOWS), ns * SUBLANES, ns * _SAMPLE_EXT_ROWS), np.float32)
    for s in range(ns):
        band[:, s * SUBLANES:(s + 1) * SUBLANES, s * _SAMPLE_EXT_ROWS:(s + 1) * _SAMPLE_EXT_ROWS] = one
    band = np.tile(band, (1, 1, 2))
    once = pl.Buffered(1)
    out = pl.pallas_call(
        _pool_sample_kernel,
        grid=(db // ns,),
        in_specs=[pl.BlockSpec((ns * _SAMPLE_EXT_ROWS, pw), lambda i: (i, 0)),
                  pl.BlockSpec(band.shape, lambda i: (0, 0, 0), pipeline_mode=once),
                  pl.BlockSpec((ng, gd, gd), lambda i: (0, 0, 0), pipeline_mode=once),
                  pl.BlockSpec((1, pw), lambda i: (0, 0))],
        out_specs=pl.BlockSpec((ns * SUBLANES, pw), lambda i: (i, 0)),
        out_shape=jax.ShapeDtypeStruct((db * SUBLANES, pw), BF16),
        compiler_params=_params("parallel"),
        name="pool_sample",
    )(ext, jnp.asarray(band, BF16), w_pool, pool_scale)
    return out.reshape(db, SUBLANES, pw)[:, :ds]


def _accumulate(o_ref, terms, col_chunk=1024):
    n = o_ref.shape[1]
    cn = min(n, col_chunk)
    for c in range(n // cn):
        cols = slice(c * cn, (c + 1) * cn)
        part = _dot(terms[0][0][...], terms[0][1][:, cols])
        for a_ref, w_ref in terms[1:]:
            part += _dot(a_ref[...], w_ref[:, cols])
        o_ref[:, cols] += part


def _outproj_kernel(a1_ref, a2_ref, w1_ref, w2_ref, x_ref, gpost_ref, gpre_ref, h_ref, hn_ref):
    kk = pl.program_id(1)

    @pl.when(kk == 0)
    def _():
        h_ref[...] = jnp.zeros_like(h_ref)

    _accumulate(h_ref, [(a1_ref, w1_ref), (a2_ref, w2_ref)])

    @pl.when(kk == pl.num_programs(1) - 1)
    def _():
        gpost = gpost_ref[...]
        gpre = gpre_ref[...]

        def finish(rows):
            h = x_ref[rows, :] + _rms(h_ref[rows, :], gpost)
            h_ref[rows, :] = h
            hn_ref[rows, :] = _rms(h, gpre).astype(BF16)

        _for_row_chunks(h_ref.shape[0], 32, finish)


def _out_projection(a1, a2, w_out, x, g_post, g_pre, tm_pref=512, tk_pref=512):
    m, k1 = a1.shape
    d = w_out.shape[1]
    tm = _tile(m, tm_pref)
    tk = _tile(k1, tk_pref)
    nk = k1 // tk
    return pl.pallas_call(
        _outproj_kernel,
        grid=(m // tm, nk),
        in_specs=[pl.BlockSpec((tm, tk), lambda i, k: (i, k)),
                  pl.BlockSpec((tm, tk), lambda i, k: (i, k)),
                  pl.BlockSpec((tk, d), lambda i, k: (k, 0)),
                  pl.BlockSpec((tk, d), lambda i, k: (k + nk, 0)),
                  pl.BlockSpec((tm, d), lambda i, k: (i, 0), pipeline_mode=pl.Buffered(1)),
                  pl.BlockSpec((1, d), lambda i, k: (0, 0)),
                  pl.BlockSpec((1, d), lambda i, k: (0, 0))],
        out_specs=[pl.BlockSpec((tm, d), lambda i, k: (i, 0)),
                   pl.BlockSpec((tm, d), lambda i, k: (i, 0))],
        out_shape=[jax.ShapeDtypeStruct((m, d), F32), jax.ShapeDtypeStruct((m, d), BF16)],
        compiler_params=_params("parallel", "arbitrary"),
        name="out_projection",
    )(a1, a2, w_out, w_out, x, g_post, g_pre)


def _up_kernel(hn_ref, w_ref, *rest, cast_along):
    if cast_along:
        other_ref, o_ref, other_out_ref, wb_ref = rest
        other_out_ref[...] = other_ref[...].astype(BF16)
    else:
        o_ref, wb_ref = rest

    @pl.when(pl.program_id(1) == 0)
    def _():
        def cast(rows):
            wb_ref[rows, :] = w_ref[rows, :].astype(BF16)

        _for_row_chunks(w_ref.shape[0], 256, cast)

    u = jnp.maximum(_dot(hn_ref[...], wb_ref[...]), 0.0)
    o_ref[...] = (u * u).astype(o_ref.dtype)


def _mlp_up(hn, w_up, cast_along=None, tm_pref=1024, tn_pref=512):
    m, d = hn.shape
    f = w_up.shape[1]
    tm = _tile(m, tm_pref)
    tn = _tile(f, tn_pref)
    ni = m // tm
    in_specs = [pl.BlockSpec((tm, d), lambda j, i: (i, 0)),
                pl.BlockSpec((d, tn), lambda j, i: (0, j))]
    out_specs = [pl.BlockSpec((tm, tn), lambda j, i: (i, j))]
    out_shape = [jax.ShapeDtypeStruct((m, f), BF16)]
    args = [hn, w_up]
    if cast_along is not None:
        rows, cols = cast_along.shape
        slab = rows // ((f // tn) * ni)
        assert slab * (f // tn) * ni == rows and slab % 16 == 0
        slab_spec = pl.BlockSpec((slab, cols), lambda j, i: (j * ni + i, 0))
        in_specs.append(slab_spec)
        out_specs.append(slab_spec)
        out_shape.append(jax.ShapeDtypeStruct((rows, cols), BF16))
        args.append(cast_along)
    out = pl.pallas_call(
        functools.partial(_up_kernel, cast_along=cast_along is not None),
        grid=(f // tn, ni),
        in_specs=in_specs,
        out_specs=out_specs,
        out_shape=out_shape,
        scratch_shapes=[pltpu.VMEM((d, tn), BF16)],
        compiler_params=_params("parallel", "arbitrary"),
        name="mlp_up",
    )(*args)
    return out if cast_along is not None else out[0]


def _down_kernel(u_ref, w_ref, h_ref, g_ref, y_ref):
    kk = pl.program_id(1)

    @pl.when(kk == 0)
    def _():
        y_ref[...] = jnp.zeros_like(y_ref)

    _accumulate(y_ref, [(u_ref, w_ref)])

    @pl.when(kk == pl.num_programs(1) - 1)
    def _():
        g = g_ref[...]

        def finish(rows):
            y_ref[rows, :] = h_ref[rows, :] + _rms(y_ref[rows, :], g)

        _for_row_chunks(y_ref.shape[0], 32, finish)


def _mlp_down(u2, w_down, h, g_post, tm_pref=512, tk_pref=1024):
    m, f = u2.shape
    d = w_down.shape[1]
    tm = _tile(m, tm_pref)
    tk = _tile(f, tk_pref)
    return pl.pallas_call(
        _down_kernel,
        grid=(m // tm, f // tk),
        in_specs=[pl.BlockSpec((tm, tk), lambda i, k: (i, k)),
                  pl.BlockSpec((tk, d), lambda i, k: (k, 0)),
                  pl.BlockSpec((tm, d), lambda i, k: (i, 0), pipeline_mode=pl.Buffered(1)),
                  pl.BlockSpec((1, d), lambda i, k: (0, 0))],
        out_specs=pl.BlockSpec((tm, d), lambda i, k: (i, 0)),
        out_shape=jax.ShapeDtypeStruct((m, d), F32),
        compiler_params=_params("parallel", "arbitrary"),
        name="mlp_down",
    )(u2, w_down, h, g_post)


def _finish_layer(x, attn, pool, w_out, g_mix_post, g_mlp_pre, w_up, w_down, g_mlp_post):
    h, hn = _out_projection(attn, pool, w_out, x, g_mix_post, g_mlp_pre)
    if w_down.dtype == BF16:
        u2 = _mlp_up(hn, w_up)
    else:
        u2, w_down = _mlp_up(hn, w_up, cast_along=w_down)
    return _mlp_down(u2, w_down, h, g_mlp_post), w_down


def kernel(x_prompt, x_sample, cache_k, cache_v, state_pool, page_table, meta, g_mix_pre, w_in, sb_bias,
           w_pool, pool_scale, w_out, g_mix_post, g_mlp_pre, w_up, w_down, g_mlp_post):
    depth = w_in.shape[0]
    assert depth == 1, "meta rows skip the MLP, which is only valid for a single layer"
    b, s, d = x_prompt.shape
    db, ds, _ = x_sample.shape
    nh = cache_k.shape[3]
    sbw = nh * HEAD_DIM
    pw = w_pool.shape[1] * w_pool.shape[2]
    layer = 0

    w_in_b = w_in[layer].astype(BF16)
    w_out_b = w_out[layer].astype(BF16)
    w_pool_b = w_pool[layer].astype(BF16)
    g_pre = g_mix_pre[layer][None, :]
    scale_row = pool_scale[layer][None, :]
    mlp = (w_out_b, g_mix_post[layer][None, :], g_mlp_pre[layer][None, :], w_up[layer])
    g_post = g_mlp_post[layer][None, :]
    bias = sb_bias[layer].astype(F32) * LOG2E

    xp = x_prompt.reshape(b * s, d)
    q_p, k_p, v_p, p_p = _in_projection(xp, g_pre, w_in_b, sbw, pw)
    xs = x_sample.reshape(db * ds, d)
    n_s = db * ds
    q_sm, k_sm, v_sm, p_sm = _in_projection(jnp.concatenate([xs, meta.astype(F32)], axis=0), g_pre, w_in_b, sbw, pw,
                                            tm_pref=n_s + N_META)
    q_s, k_s, v_s, p_s = q_sm[:n_s], k_sm[:n_s], v_sm[:n_s], p_sm[:n_s]
    k_m, v_m, p_m = k_sm[n_s:], v_sm[n_s:], p_sm[n_s:]

    attn_p = _prompt_attention(q_p, k_p, v_p, k_m, v_m, bias, b, s)
    pool_p = _pool_prompt(p_p, p_m, w_pool_b, scale_row, s)
    y_p, w_down_b = _finish_layer(xp, attn_p, pool_p, *mlp, w_down[layer], g_post)

    attn_s = _sample_attention(q_s.reshape(db, ds, sbw), k_s.reshape(db, ds, sbw), v_s.reshape(db, ds, sbw),
                               cache_k[layer], cache_v[layer], page_table, bias)
    pool_s = _pool_sample(state_pool[layer], p_s.reshape(db, ds, pw), w_pool_b, scale_row)
    y_s, _ = _finish_layer(xs, attn_s.reshape(n_s, sbw), pool_s.reshape(n_s, pw), *mlp, w_down_b, g_post)

    def with_meta(a_meta, a):
        full = jnp.concatenate([jnp.broadcast_to(a_meta[None], (b, N_META, sbw)), a.reshape(b, s, sbw)], axis=1)
        return full.reshape(1, b, N_META + s, nh, HEAD_DIM)

    assert s >= POOL_BUF
    pool_prompt = p_p.reshape(b, s, pw)[:, -POOL_BUF:]
    pool_sample = jnp.concatenate([state_pool[layer], p_s.reshape(db, ds, pw)], axis=1)[:, -POOL_BUF:]
    return (y_p.reshape(b, s, d), y_s.reshape(db, ds, d),
            with_meta(k_m, k_p), with_meta(v_m, v_p), pool_prompt[None],
            k_s.reshape(1, db, ds, nh, HEAD_DIM), v_s.reshape(1, db, ds, nh, HEAD_DIM), pool_sample[None])
```

```python
import functools

import numpy as np
import jax
import jax.numpy as jnp
from jax import lax
from jax.experimental import pallas as pl
from jax.experimental.pallas import tpu as pltpu

F32 = jnp.float32
BF16 = jnp.bfloat16

HEAD_DIM = 128
N_META = 16
POOL_WINDOWS = (2, 4, 8, 16)
POOL_BUF = max(POOL_WINDOWS) - 1
EPS = 1e-6
CAST_CHUNK_ELEMS = 256 * 1024
LOG2E = 1.4426950408889634
LANES = 128
SUBLANES = 8
VMEM_LIMIT_BYTES = 56 * 1024 * 1024


def _tile(n, pref):
    if n <= pref:
        return n
    t = pref
    while n % t:
        t //= 2
    return t


def _params(*sem):
    return pltpu.CompilerParams(dimension_semantics=sem, vmem_limit_bytes=VMEM_LIMIT_BYTES)


def _dot(a, b):
    return jnp.dot(a, b, preferred_element_type=F32)


def _dot_nt(a, b):
    return lax.dot_general(a, b, (((1,), (1,)), ((), ())), preferred_element_type=F32)


def _split_bf16(x):
    hi = x.astype(BF16)
    lo = (x - hi.astype(F32)).astype(BF16)
    return hi, lo


def _rms(x, g):
    ms = jnp.mean(x * x, axis=-1, keepdims=True)
    return x * lax.rsqrt(ms + EPS) * g


def _for_row_chunks(n_rows, chunk, fn):
    chunk = _tile(n_rows, chunk)

    def body(c, carry):
        fn(pl.ds(pl.multiple_of(c * chunk, chunk), chunk))
        return carry

    lax.fori_loop(0, n_rows // chunk, body, 0)


def _inproj_kernel(x_ref, g_ref, wq_ref, wk_ref, wv_ref, wp_ref, q_ref, k_ref, v_ref, p_ref, xn_ref, *, q_scale):
    @pl.when(pl.program_id(1) == 0)
    def _():
        g = g_ref[...]

        def norm(rows):
            xn_ref[rows, :] = _rms(x_ref[rows, :], g).astype(BF16)

        _for_row_chunks(x_ref.shape[0], 64, norm)

    xn = xn_ref[...]
    q_ref[...] = (_dot(xn, wq_ref[...]) * q_scale).astype(q_ref.dtype)
    k_ref[...] = _dot(xn, wk_ref[...])
    v_ref[...] = _dot(xn, wv_ref[...])
    p_ref[...] = _dot(xn, wp_ref[...])


def _in_projection(x, g, w_in, sbw, pw, tm_pref=512, tn_pref=256):
    m, d = x.shape
    tm = _tile(m, tm_pref)
    tn = _tile(min(sbw, pw), tn_pref)
    assert sbw % tn == 0 and pw % tn == 0 and sbw == pw
    nj = sbw // tn

    def wspec(c):
        return pl.BlockSpec((d, tn), lambda i, j: (0, j + c * nj))

    ospec = pl.BlockSpec((tm, tn), lambda i, j: (i, j))
    return pl.pallas_call(
        functools.partial(_inproj_kernel, q_scale=HEAD_DIM ** -0.5 * LOG2E),
        grid=(m // tm, nj),
        in_specs=[pl.BlockSpec((tm, d), lambda i, j: (i, 0)),
                  pl.BlockSpec((1, d), lambda i, j: (0, 0)),
                  wspec(0), wspec(1), wspec(2), wspec(3)],
        out_specs=[ospec, ospec, ospec, ospec],
        out_shape=[jax.ShapeDtypeStruct((m, sbw), BF16),
                   jax.ShapeDtypeStruct((m, sbw), F32),
                   jax.ShapeDtypeStruct((m, sbw), F32),
                   jax.ShapeDtypeStruct((m, pw), F32)],
        scratch_shapes=[pltpu.VMEM((tm, d), BF16)],
        compiler_params=_params("parallel", "arbitrary"),
        name="in_projection",
    )(x, g, w_in, w_in, w_in, w_in)


def _suffix_matrix(tk, copies=1):
    j = np.arange(tk)[:, None]
    s = np.arange(tk)[None, :]
    return jnp.asarray(np.tile((j > s).astype(np.float32), (copies, 1)), BF16)


def _sb_weights(z, valid, u, carry, stack=False):
    r, tk = z.shape
    sp = jnp.maximum(z, 0.0) + jnp.log2(1.0 + jnp.exp2(-jnp.abs(z)))
    if valid is not None:
        sp = jnp.where(valid, sp, 0.0)
    hi, lo = _split_bf16(sp)
    if stack:
        su = _dot(jnp.concatenate([hi, lo], axis=0), u)
        su = su[:r] + su[r:]
    else:
        su = _dot(jnp.concatenate([hi, lo], axis=1), u)
    later = su + jnp.concatenate([carry] * (tk // LANES), axis=1)
    w = jnp.exp2(z - sp - later)
    if valid is not None:
        w = jnp.where(valid, w, 0.0)
    total = jnp.broadcast_to(jnp.sum(sp, axis=1, keepdims=True), (r, LANES))
    return w, carry + total


def _prompt_attn_kernel(bias_ref, q_ref, k_ref, v_ref, km_ref, vm_ref, u1_ref, u1m_ref, *rest, tq, tk, n_cast):
    cast_in = rest[:n_cast]
    o_ref = rest[n_cast]
    cast_out = rest[n_cast + 1:2 * n_cast + 1]
    kb_ref, vb_ref, carry_ref, acc_ref = rest[2 * n_cast + 1:]
    h = pl.program_id(1)
    qi = pl.program_id(2)

    for src, dst in zip(cast_in, cast_out):
        def cast(rows, src=src, dst=dst):
            dst[rows, :] = src[rows, :].astype(BF16)

        _for_row_chunks(src.shape[0], max(16, CAST_CHUNK_ELEMS // src.shape[1]), cast)

    @pl.when(qi == 0)
    def _():
        kb_ref[...] = k_ref[...].astype(BF16)
        vb_ref[...] = v_ref[...].astype(BF16)

    bias = bias_ref[h]
    carry_ref[...] = jnp.zeros_like(carry_ref)
    acc_ref[...] = jnp.zeros_like(acc_ref)

    def block(r0, kblk, vblk, valid, u):
        z = _dot_nt(q_ref[r0:, :], kblk) + bias
        w, carry = _sb_weights(z, valid, u, carry_ref[r0:, :])
        carry_ref[r0:, :] = carry
        acc_ref[r0:, :] += _dot(w.astype(BF16), vblk)

    for d in reversed(range(tq // tk)):
        r0 = d * tk
        off = pl.multiple_of(qi * tq + r0, tk)
        row = lax.broadcasted_iota(jnp.int32, (tq - r0, tk), 0)
        col = lax.broadcasted_iota(jnp.int32, (tq - r0, tk), 1)
        block(r0, kb_ref[pl.ds(off, tk), :], vb_ref[pl.ds(off, tk), :], col < row, u1_ref[...])

    n_before = qi * (tq // tk)

    def body(it, c):
        off = pl.multiple_of((n_before - 1 - it) * tk, tk)
        block(0, kb_ref[pl.ds(off, tk), :], vb_ref[pl.ds(off, tk), :], None, u1_ref[...])
        return c

    lax.fori_loop(0, n_before, body, 0)

    colm = lax.broadcasted_iota(jnp.int32, (tq, LANES), 1)
    block(0, km_ref[...].astype(BF16), vm_ref[...].astype(BF16), colm < N_META, u1m_ref[...])
    o_ref[...] = acc_ref[...].astype(o_ref.dtype)


def _prompt_attention(q, k, v, k_meta, v_meta, bias, batch, seq, cast_along=(), tq_pref=2048, tk_pref=256):
    m, sbw = q.shape
    nh = sbw // HEAD_DIM
    tq = _tile(seq, tq_pref)
    tk = _tile(tq, tk_pref)
    nq = seq // tq
    pad = ((0, LANES - N_META), (0, 0))
    km = jnp.pad(k_meta, pad)
    vm = jnp.pad(v_meta, pad)
    kern = functools.partial(_prompt_attn_kernel, tq=tq, tk=tk, n_cast=len(cast_along))
    qspec = pl.BlockSpec((tq, HEAD_DIM), lambda b, h, i: (b * nq + i, h))
    kvspec = pl.BlockSpec((seq, HEAD_DIM), lambda b, h, i: (b, h))
    mspec = pl.BlockSpec((LANES, HEAD_DIM), lambda b, h, i: (0, h))
    n_steps = batch * nh * nq
    cast_specs = []
    for w in cast_along:
        rows, cols = w.shape
        slab = rows // n_steps
        assert slab * n_steps == rows and slab % 16 == 0
        cast_specs.append(pl.BlockSpec((slab, cols), lambda b, h, i: ((b * nh + h) * nq + i, 0)))
    out = pl.pallas_call(
        kern,
        grid=(batch, nh, nq),
        in_specs=[pl.BlockSpec(memory_space=pltpu.SMEM), qspec, kvspec, kvspec, mspec, mspec,
                  pl.BlockSpec((2 * tk, tk), lambda b, h, i: (0, 0)),
                  pl.BlockSpec((2 * LANES, LANES), lambda b, h, i: (0, 0))] + cast_specs,
        out_specs=[qspec] + cast_specs,
        out_shape=[jax.ShapeDtypeStruct((m, sbw), BF16)] + [jax.ShapeDtypeStruct(w.shape, BF16) for w in cast_along],
        scratch_shapes=[pltpu.VMEM((seq, HEAD_DIM), BF16), pltpu.VMEM((seq, HEAD_DIM), BF16),
                        pltpu.VMEM((tq, LANES), F32), pltpu.VMEM((tq, HEAD_DIM), F32)],
        compiler_params=_params("parallel", "parallel", "arbitrary"),
        name="prompt_attention",
    )(bias, q, k, v, km, vm, _suffix_matrix(tk, 2), _suffix_matrix(LANES, 2), *cast_along)
    return out[0], tuple(out[1:])


HEAD_TILE = SUBLANES
PAGES_PER_BLOCK = 4


def _sample_attn_kernel(pt_ref, q_ref, kn_ref, vn_ref, bias_ref, u1_ref, ug_ref, mask_ref, sel_ref, selt_ref, *rest,
                        group):
    kc_refs = rest[:group]
    vc_refs = rest[group:2 * group]
    o_ref, carry_ref, acc_ref = rest[2 * group:]
    del pt_ref
    j = pl.program_id(1)
    n_half, hr = q_ref.shape[:2]
    bias = bias_ref[...]

    def attend(k_of, v_of, n_blocks, valid, u):
        r = k_of(0, 0).shape[0]
        m = mask_ref[:, :r]
        halves = []
        for half in range(n_half):
            kcat = jnp.concatenate([k_of(i, half) for i in range(n_blocks)], axis=0)
            zt = _dot_nt(q_ref[half], kcat)
            for i in range(n_blocks):
                halves.extend(_split_bf16(zt[:, i * r:(i + 1) * r] * m))
        zz = _dot(jnp.concatenate(halves, axis=0), sel_ref[:r, :])
        zs = []
        for i in range(n_blocks):
            parts = []
            for half in range(n_half):
                base = (half * n_blocks + i) * 2 * hr
                parts.append(zz[base:base + hr] + zz[base + hr:base + 2 * hr])
            zs.append(jnp.concatenate(parts, axis=0) + bias)
        w, carry = _sb_weights(jnp.concatenate(zs, axis=1), valid, u, carry_ref[...], stack=True)
        carry_ref[...] = carry
        w = w.astype(BF16)
        wrows = jnp.concatenate([w[half * hr:(half + 1) * hr, i * LANES:(i + 1) * LANES]
                                 for half in range(n_half) for i in range(n_blocks)], axis=0)
        wide = _dot(wrows, selt_ref[:, :r])
        for half in range(n_half):
            lhs = jnp.concatenate(
                [(wide[(half * n_blocks + i) * hr:(half * n_blocks + i + 1) * hr] * m).astype(BF16)
                 for i in range(n_blocks)], axis=1)
            vcat = jnp.concatenate([v_of(i, half) for i in range(n_blocks)], axis=0)
            acc_ref[half] += _dot(lhs, vcat)

    @pl.when(j == 0)
    def _():
        carry_ref[...] = jnp.zeros_like(carry_ref)
        acc_ref[...] = jnp.zeros_like(acc_ref)
        t = (lax.broadcasted_iota(jnp.int32, (n_half * hr, LANES), 0) % hr) // HEAD_TILE
        key = lax.broadcasted_iota(jnp.int32, (n_half * hr, LANES), 1)
        attend(lambda i, half: kn_ref[half].astype(BF16), lambda i, half: vn_ref[half].astype(BF16), 1,
               key < t, u1_ref[...])

    def half_page(ref, half):
        rows = ref.shape[0] // n_half
        x = ref[pl.ds(half, rows, stride=n_half), :, :]
        return x.reshape(rows * HEAD_TILE, HEAD_DIM).astype(BF16)

    for first in range(0, group, PAGES_PER_BLOCK):
        last = first + PAGES_PER_BLOCK - 1
        attend(lambda i, half: half_page(kc_refs[last - i], half),
               lambda i, half: half_page(vc_refs[last - i], half), PAGES_PER_BLOCK, None, ug_ref[...])

    @pl.when(j == pl.num_programs(1) - 1)
    def _():
        o_ref[...] = acc_ref[...].astype(o_ref.dtype)


def _sample_attention(q, k_new, v_new, cache_k, cache_v, page_table, bias, group_pref=8):
    db, ds, sbw = q.shape
    nh = sbw // HEAD_DIM
    n_pages = page_table.shape[1]
    n_phys, page = cache_k.shape[:2]
    group = _tile(n_pages, group_pref)
    assert page == LANES and ds <= page and nh % HEAD_TILE == 0 and group % PAGES_PER_BLOCK == 0
    n_half = nh // HEAD_TILE
    hr = ds * HEAD_TILE
    flat = page * HEAD_TILE

    def by_half(a):
        a = a.reshape(db, ds, n_half, HEAD_TILE, HEAD_DIM).transpose(0, 2, 1, 3, 4)
        return a.reshape(db, n_half, hr, HEAD_DIM)

    mask = (np.arange(hr)[:, None] % HEAD_TILE == np.arange(flat)[None, :] % HEAD_TILE).astype(np.float32)
    sel = (np.arange(flat)[:, None] // HEAD_TILE == np.arange(LANES)[None, :]).astype(np.float32)
    bias_rows = jnp.broadcast_to(bias.astype(F32).reshape(n_half, 1, HEAD_TILE, 1),
                                 (n_half, ds, HEAD_TILE, LANES)).reshape(n_half * hr, LANES)
    pt = page_table.reshape(-1).astype(jnp.int32)

    def const(shape):
        return pl.BlockSpec(shape, lambda n, j, pt: (0,) * len(shape))

    seq_spec = pl.BlockSpec((None, n_half, hr, HEAD_DIM), lambda n, j, pt: (n, 0, 0, 0))

    def page_spec(c):
        def imap(n, j, pt):
            return (pt[n * n_pages + (n_pages - 1 - (j * group + c))], 0, 0, 0)
        return pl.BlockSpec((None, page * n_half, HEAD_TILE, HEAD_DIM), imap)

    kern = functools.partial(_sample_attn_kernel, group=group)
    ck = cache_k.reshape(n_phys, page * n_half, HEAD_TILE, HEAD_DIM)
    cv = cache_v.reshape(n_phys, page * n_half, HEAD_TILE, HEAD_DIM)
    out = pl.pallas_call(
        kern,
        grid_spec=pltpu.PrefetchScalarGridSpec(
            num_scalar_prefetch=1,
            grid=(db, n_pages // group),
            in_specs=[seq_spec, seq_spec, seq_spec, const((n_half * hr, LANES)), const((page, page)),
                      const((PAGES_PER_BLOCK * page, PAGES_PER_BLOCK * page)),
                      const((hr, flat)), const((flat, LANES)), const((LANES, flat))]
                     + [page_spec(c) for c in range(group)] * 2,
            out_specs=seq_spec,
            scratch_shapes=[pltpu.VMEM((n_half * hr, LANES), F32), pltpu.VMEM((n_half, hr, HEAD_DIM), F32)]),
        out_shape=jax.ShapeDtypeStruct((db, n_half, hr, HEAD_DIM), BF16),
        compiler_params=_params("parallel", "arbitrary"),
        name="sample_attention",
    )(pt, by_half(q), by_half(k_new), by_half(v_new),
      bias_rows, _suffix_matrix(page), _suffix_matrix(PAGES_PER_BLOCK * page), jnp.asarray(mask), jnp.asarray(sel, BF16),
      jnp.asarray(sel.T, BF16), *([ck] * group), *([cv] * group))
    out = out.reshape(db, n_half, ds, HEAD_TILE, HEAD_DIM).transpose(0, 2, 1, 3, 4)
    return out.reshape(db, ds, sbw)


def _pool_tail(b_ref, ext, w_ref, s_ref, o_ref):
    ng, gd = w_ref.shape[:2]
    for g in range(ng):
        cols = slice(g * gd, (g + 1) * gd)
        hi, lo = _split_bf16(ext[:, cols])
        d = _dot(b_ref[g], jnp.concatenate([hi, lo], axis=0))
        o_ref[:, cols] = (_dot(d.astype(BF16), w_ref[g]) * s_ref[:, cols]).astype(o_ref.dtype)


def _pool_prompt_kernel(p_ref, halo_ref, meta_ref, b_ref, w_ref, s_ref, o_ref, *, tiles_per_seq):
    first = (pl.program_id(0) % tiles_per_seq) == 0
    halo = jnp.where(first, meta_ref[...], halo_ref[...])
    _pool_tail(b_ref, jnp.concatenate([halo, p_ref[...]], axis=0), w_ref, s_ref, o_ref)


def _pool_sample_kernel(e_ref, b_ref, w_ref, s_ref, o_ref):
    _pool_tail(b_ref, e_ref[...], w_ref, s_ref, o_ref)


def _window_rows(n_out, n_in, first):
    b = np.zeros((len(POOL_WINDOWS), n_out, n_in), np.float32)
    for g, w in enumerate(POOL_WINDOWS):
        for r in range(n_out):
            b[g, r, first + r - w + 1:first + r + 1] += 1.0 / w
            b[g, r, first + r] -= 1.0
    return b


def _pool_prompt(p, p_meta, w_pool, pool_scale, seq, tm_pref=512):
    m, pw = p.shape
    ng, gd, _ = w_pool.shape
    tm = _tile(seq, tm_pref)
    assert tm % LANES == 0
    hb = tm // LANES
    meta = jnp.pad(p_meta, ((LANES - N_META, 0), (0, 0)))
    band = np.tile(_window_rows(tm, LANES + tm, LANES), (1, 1, 2))
    kern = functools.partial(_pool_prompt_kernel, tiles_per_seq=seq // tm)
    once = pl.Buffered(1)
    return pl.pallas_call(
        kern,
        grid=(m // tm,),
        in_specs=[pl.BlockSpec((tm, pw), lambda i: (i, 0)),
                  pl.BlockSpec((LANES, pw), lambda i: (jnp.maximum(i * hb - 1, 0), 0)),
                  pl.BlockSpec((LANES, pw), lambda i: (0, 0), pipeline_mode=once),
                  pl.BlockSpec(band.shape, lambda i: (0, 0, 0), pipeline_mode=once),
                  pl.BlockSpec((ng, gd, gd), lambda i: (0, 0, 0), pipeline_mode=once),
                  pl.BlockSpec((1, pw), lambda i: (0, 0))],
        out_specs=pl.BlockSpec((tm, pw), lambda i: (i, 0)),
        out_shape=jax.ShapeDtypeStruct((m, pw), BF16),
        compiler_params=_params("parallel"),
        name="pool_prompt",
    )(p, p, meta, jnp.asarray(band, BF16), w_pool, pool_scale)


_SAMPLE_EXT_ROWS = 32


def _pool_sample(state, p_new, w_pool, pool_scale, seqs_pref=16):
    db, ds, pw = p_new.shape
    ng, gd, _ = w_pool.shape
    ns = _tile(db, seqs_pref)
    assert POOL_BUF + ds <= _SAMPLE_EXT_ROWS and ds <= SUBLANES
    ext = jnp.concatenate(
        [state, p_new, jnp.zeros((db, _SAMPLE_EXT_ROWS - POOL_BUF - ds, pw), F32)], axis=1)
    ext = ext.reshape(db * _SAMPLE_EXT_ROWS, pw)
    one = np.zeros((len(POOL_WINDOWS), SUBLANES, _SAMPLE_EXT_ROWS), np.float32)
    one[:, :ds] = _window_rows(ds, _SAMPLE_EXT_ROWS, POOL_BUF)
    band = np.zeros((len(POOL_WINDOWS), ns * SUBLANES, ns * _SAMPLE_EXT_ROWS), np.float32)
    for s in range(ns):
        band[:, s * SUBLANES:(s + 1) * SUBLANES, s * _SAMPLE_EXT_ROWS:(s + 1) * _SAMPLE_EXT_ROWS] = one
    band = np.tile(band, (1, 1, 2))
    once = pl.Buffered(1)
    out = pl.pallas_call(
        _pool_sample_kernel,
        grid=(db // ns,),
        in_specs=[pl.BlockSpec((ns * _SAMPLE_EXT_ROWS, pw), lambda i: (i, 0)),
                  pl.BlockSpec(band.shape, lambda i: (0, 0, 0), pipeline_mode=once),
                  pl.BlockSpec((ng, gd, gd), lambda i: (0, 0, 0), pipeline_mode=once),
                  pl.BlockSpec((1, pw), lambda i: (0, 0))],
        out_specs=pl.BlockSpec((ns * SUBLANES, pw), lambda i: (i, 0)),
        out_shape=jax.ShapeDtypeStruct((db * SUBLANES, pw), BF16),
        compiler_params=_params("parallel"),
        name="pool_sample",
    )(ext, jnp.asarray(band, BF16), w_pool, pool_scale)
    return out.reshape(db, SUBLANES, pw)[:, :ds]


def _accumulate(o_ref, terms, col_chunk=1024):
    n = o_ref.shape[1]
    cn = min(n, col_chunk)
    for c in range(n // cn):
        cols = slice(c * cn, (c + 1) * cn)
        part = _dot(terms[0][0][...], terms[0][1][:, cols])
        for a_ref, w_ref in terms[1:]:
            part += _dot(a_ref[...], w_ref[:, cols])
        o_ref[:, cols] += part


def _outproj_kernel(a1_ref, a2_ref, w1_ref, w2_ref, x_ref, gpost_ref, gpre_ref, h_ref, hn_ref):
    kk = pl.program_id(1)

    @pl.when(kk == 0)
    def _():
        h_ref[...] = jnp.zeros_like(h_ref)

    _accumulate(h_ref, [(a1_ref, w1_ref), (a2_ref, w2_ref)])

    @pl.when(kk == pl.num_programs(1) - 1)
    def _():
        gpost = gpost_ref[...]
        gpre = gpre_ref[...]

        def finish(rows):
            h = x_ref[rows, :] + _rms(h_ref[rows, :], gpost)
            h_ref[rows, :] = h
            hn_ref[rows, :] = _rms(h, gpre).astype(BF16)

        _for_row_chunks(h_ref.shape[0], 32, finish)


def _out_projection(a1, a2, w_out, x, g_post, g_pre, tm_pref=512, tk_pref=512):
    m, k1 = a1.shape
    d = w_out.shape[1]
    tm = _tile(m, tm_pref)
    tk = _tile(k1, tk_pref)
    nk = k1 // tk
    return pl.pallas_call(
        _outproj_kernel,
        grid=(m // tm, nk),
        in_specs=[pl.BlockSpec((tm, tk), lambda i, k: (i, k)),
                  pl.BlockSpec((tm, tk), lambda i, k: (i, k)),
                  pl.BlockSpec((tk, d), lambda i, k: (k, 0)),
                  pl.BlockSpec((tk, d), lambda i, k: (k + nk, 0)),
                  pl.BlockSpec((tm, d), lambda i, k: (i, 0), pipeline_mode=pl.Buffered(1)),
                  pl.BlockSpec((1, d), lambda i, k: (0, 0)),
                  pl.BlockSpec((1, d), lambda i, k: (0, 0))],
        out_specs=[pl.BlockSpec((tm, d), lambda i, k: (i, 0)),
                   pl.BlockSpec((tm, d), lambda i, k: (i, 0))],
        out_shape=[jax.ShapeDtypeStruct((m, d), F32), jax.ShapeDtypeStruct((m, d), BF16)],
        compiler_params=_params("parallel", "arbitrary"),
        name="out_projection",
    )(a1, a2, w_out, w_out, x, g_post, g_pre)


def _up_kernel(hn_ref, w_ref, o_ref):
    u = jnp.maximum(_dot(hn_ref[...], w_ref[...]), 0.0)
    o_ref[...] = (u * u).astype(o_ref.dtype)


def _mlp_up(hn, w_up, tm_pref=1024, tn_pref=512):
    m, d = hn.shape
    f = w_up.shape[1]
    tm = _tile(m, tm_pref)
    tn = _tile(f, tn_pref)
    return pl.pallas_call(
        _up_kernel,
        grid=(m // tm, f // tn),
        in_specs=[pl.BlockSpec((tm, d), lambda i, j: (i, 0)),
                  pl.BlockSpec((d, tn), lambda i, j: (0, j))],
        out_specs=pl.BlockSpec((tm, tn), lambda i, j: (i, j)),
        out_shape=jax.ShapeDtypeStruct((m, f), BF16),
        compiler_params=_params("parallel", "arbitrary"),
        name="mlp_up",
    )(hn, w_up)


def _down_kernel(u_ref, w_ref, h_ref, g_ref, y_ref):
    kk = pl.program_id(1)

    @pl.when(kk == 0)
    def _():
        y_ref[...] = jnp.zeros_like(y_ref)

    _accumulate(y_ref, [(u_ref, w_ref)])

    @pl.when(kk == pl.num_programs(1) - 1)
    def _():
        g = g_ref[...]

        def finish(rows):
            y_ref[rows, :] = h_ref[rows, :] + _rms(y_ref[rows, :], g)

        _for_row_chunks(y_ref.shape[0], 32, finish)


def _mlp_down(u2, w_down, h, g_post, tm_pref=512, tk_pref=1024):
    m, f = u2.shape
    d = w_down.shape[1]
    tm = _tile(m, tm_pref)
    tk = _tile(f, tk_pref)
    return pl.pallas_call(
        _down_kernel,
        grid=(m // tm, f // tk),
        in_specs=[pl.BlockSpec((tm, tk), lambda i, k: (i, k)),
                  pl.BlockSpec((tk, d), lambda i, k: (k, 0)),
                  pl.BlockSpec((tm, d), lambda i, k: (i, 0), pipeline_mode=pl.Buffered(1)),
                  pl.BlockSpec((1, d), lambda i, k: (0, 0))],
        out_specs=pl.BlockSpec((tm, d), lambda i, k: (i, 0)),
        out_shape=jax.ShapeDtypeStruct((m, d), F32),
        compiler_params=_params("parallel", "arbitrary"),
        name="mlp_down",
    )(u2, w_down, h, g_post)


def _finish_layer(x, attn, pool, w_out, g_mix_post, g_mlp_pre, w_up, w_down, g_mlp_post):
    h, hn = _out_projection(attn, pool, w_out, x, g_mix_post, g_mlp_pre)
    u2 = _mlp_up(hn, w_up)
    return _mlp_down(u2, w_down, h, g_mlp_post)


def kernel(x_prompt, x_sample, cache_k, cache_v, state_pool, page_table, meta, g_mix_pre, w_in, sb_bias,
           w_pool, pool_scale, w_out, g_mix_post, g_mlp_pre, w_up, w_down, g_mlp_post):
    depth = w_in.shape[0]
    assert depth == 1, "meta rows skip the MLP, which is only valid for a single layer"
    b, s, d = x_prompt.shape
    db, ds, _ = x_sample.shape
    nh = cache_k.shape[3]
    sbw = nh * HEAD_DIM
    pw = w_pool.shape[1] * w_pool.shape[2]
    layer = 0

    w_in_b = w_in[layer].astype(BF16)
    w_pool_b = w_pool[layer].astype(BF16)
    g_pre = g_mix_pre[layer][None, :]
    scale_row = pool_scale[layer][None, :]
    bias = sb_bias[layer].astype(F32) * LOG2E

    xp = x_prompt.reshape(b * s, d)
    q_p, k_p, v_p, p_p = _in_projection(xp, g_pre, w_in_b, sbw, pw)
    xs = x_sample.reshape(db * ds, d)
    n_s = db * ds
    q_sm, k_sm, v_sm, p_sm = _in_projection(jnp.concatenate([xs, meta.astype(F32)], axis=0), g_pre, w_in_b, sbw, pw,
                                            tm_pref=n_s + N_META)
    q_s, k_s, v_s, p_s = q_sm[:n_s], k_sm[:n_s], v_sm[:n_s], p_sm[:n_s]
    k_m, v_m, p_m = k_sm[n_s:], v_sm[n_s:], p_sm[n_s:]

    attn_p, (w_out_b, w_up_b, w_down_b) = _prompt_attention(
        q_p, k_p, v_p, k_m, v_m, bias, b, s, cast_along=(w_out[layer], w_up[layer], w_down[layer]))
    mlp = (w_out_b, g_mix_post[layer][None, :], g_mlp_pre[layer][None, :], w_up_b, w_down_b,
           g_mlp_post[layer][None, :])
    pool_p = _pool_prompt(p_p, p_m, w_pool_b, scale_row, s)
    y_p = _finish_layer(xp, attn_p, pool_p, *mlp)

    attn_s = _sample_attention(q_s.reshape(db, ds, sbw), k_s.reshape(db, ds, sbw), v_s.reshape(db, ds, sbw),
                               cache_k[layer], cache_v[layer], page_table, bias)
    pool_s = _pool_sample(state_pool[layer], p_s.reshape(db, ds, pw), w_pool_b, scale_row)
    y_s = _finish_layer(xs, attn_s.reshape(n_s, sbw), pool_s.reshape(n_s, pw), *mlp)

    def with_meta(a_meta, a):
        full = jnp.concatenate([jnp.broadcast_to(a_meta[None], (b, N_META, sbw)), a.reshape(b, s, sbw)], axis=1)
        return full.reshape(1, b, N_META + s, nh, HEAD_DIM)

    assert s >= POOL_BUF
    pool_prompt = p_p.reshape(b, s, pw)[:, -POOL_BUF:]
    pool_sample = jnp.concatenate([state_pool[layer], p_s.reshape(db, ds, pw)], axis=1)[:, -POOL_BUF:]
    return (y_p.reshape(b, s, d), y_s.reshape(db, ds, d),
            with_meta(k_m, k_p), with_meta(v_m, v_p), pool_prompt[None],
            k_s.reshape(1, db, ds, nh, HEAD_DIM), v_s.reshape(1, db, ds, nh, HEAD_DIM), pool_sample[None])
```

```python
import functools

import numpy as np
import jax
import jax.numpy as jnp
from jax import lax
from jax.experimental import pallas as pl
from jax.experimental.pallas import tpu as pltpu

F32 = jnp.float32
BF16 = jnp.bfloat16

HEAD_DIM = 128
N_META = 16
POOL_WINDOWS = (2, 4, 8, 16)
POOL_BUF = max(POOL_WINDOWS) - 1
EPS = 1e-6
EPILOGUE_ROWS = 128
CAST_CHUNK_ELEMS = 256 * 1024
LOG2E = 1.4426950408889634
LANES = 128
SUBLANES = 8
VMEM_LIMIT_BYTES = 56 * 1024 * 1024


def _tile(n, pref):
    if n <= pref:
        return n
    t = pref
    while n % t:
        t //= 2
    return t


def _params(*sem):
    return pltpu.CompilerParams(dimension_semantics=sem, vmem_limit_bytes=VMEM_LIMIT_BYTES)


def _dot(a, b):
    return jnp.dot(a, b, preferred_element_type=F32)


def _dot_nt(a, b):
    return lax.dot_general(a, b, (((1,), (1,)), ((), ())), preferred_element_type=F32)


def _split_bf16(x):
    hi = x.astype(BF16)
    lo = (x - hi.astype(F32)).astype(BF16)
    return hi, lo


def _rms(x, g):
    ms = jnp.mean(x * x, axis=-1, keepdims=True)
    return x * lax.rsqrt(ms + EPS) * g


def _for_row_chunks(n_rows, chunk, fn):
    chunk = _tile(n_rows, chunk)

    def body(c, carry):
        fn(pl.ds(pl.multiple_of(c * chunk, chunk), chunk))
        return carry

    lax.fori_loop(0, n_rows // chunk, body, 0)


def _inproj_kernel(x_ref, g_ref, wq_ref, wk_ref, wv_ref, wp_ref, q_ref, k_ref, v_ref, p_ref, xn_ref, *, q_scale):
    @pl.when(pl.program_id(1) == 0)
    def _():
        g = g_ref[...]

        def norm(rows):
            xn_ref[rows, :] = _rms(x_ref[rows, :], g).astype(BF16)

        _for_row_chunks(x_ref.shape[0], 64, norm)

    xn = xn_ref[...]
    q_ref[...] = (_dot(xn, wq_ref[...]) * q_scale).astype(q_ref.dtype)
    k_ref[...] = _dot(xn, wk_ref[...])
    v_ref[...] = _dot(xn, wv_ref[...])
    p_ref[...] = _dot(xn, wp_ref[...])


def _in_projection(x, g, w_in, sbw, pw, tm_pref=512, tn_pref=256):
    m, d = x.shape
    tm = _tile(m, tm_pref)
    tn = _tile(min(sbw, pw), tn_pref)
    assert sbw % tn == 0 and pw % tn == 0 and sbw == pw
    nj = sbw // tn

    def wspec(c):
        return pl.BlockSpec((d, tn), lambda i, j: (0, j + c * nj))

    ospec = pl.BlockSpec((tm, tn), lambda i, j: (i, j))
    return pl.pallas_call(
        functools.partial(_inproj_kernel, q_scale=HEAD_DIM ** -0.5 * LOG2E),
        grid=(m // tm, nj),
        in_specs=[pl.BlockSpec((tm, d), lambda i, j: (i, 0)),
                  pl.BlockSpec((1, d), lambda i, j: (0, 0)),
                  wspec(0), wspec(1), wspec(2), wspec(3)],
        out_specs=[ospec, ospec, ospec, ospec],
        out_shape=[jax.ShapeDtypeStruct((m, sbw), BF16),
                   jax.ShapeDtypeStruct((m, sbw), F32),
                   jax.ShapeDtypeStruct((m, sbw), F32),
                   jax.ShapeDtypeStruct((m, pw), F32)],
        scratch_shapes=[pltpu.VMEM((tm, d), BF16)],
        compiler_params=_params("parallel", "arbitrary"),
        name="in_projection",
    )(x, g, w_in, w_in, w_in, w_in)


def _suffix_matrix(tk, copies=1):
    j = np.arange(tk)[:, None]
    s = np.arange(tk)[None, :]
    return jnp.asarray(np.tile((j > s).astype(np.float32), (copies, 1)), BF16)


def _sb_weights(z, valid, u, carry, stack=False):
    r, tk = z.shape
    sp = jnp.maximum(z, 0.0) + jnp.log2(1.0 + jnp.exp2(-jnp.abs(z)))
    if valid is not None:
        sp = jnp.where(valid, sp, 0.0)
    hi, lo = _split_bf16(sp)
    if stack:
        su = _dot(jnp.concatenate([hi, lo], axis=0), u)
        su = su[:r] + su[r:]
    else:
        su = _dot(jnp.concatenate([hi, lo], axis=1), u)
    later = su + jnp.concatenate([carry] * (tk // LANES), axis=1)
    w = jnp.exp2(z - sp - later)
    if valid is not None:
        w = jnp.where(valid, w, 0.0)
    total = jnp.broadcast_to(jnp.sum(sp, axis=1, keepdims=True), (r, LANES))
    return w, carry + total


def _prompt_attn_kernel(bias_ref, q_ref, k_ref, v_ref, km_ref, vm_ref, u1_ref, u1m_ref, *rest, tq, tk, n_cast):
    cast_in = rest[:n_cast]
    o_ref = rest[n_cast]
    cast_out = rest[n_cast + 1:2 * n_cast + 1]
    kb_ref, vb_ref, carry_ref, acc_ref = rest[2 * n_cast + 1:]
    h = pl.program_id(1)
    qi = pl.program_id(2)

    for src, dst in zip(cast_in, cast_out):
        def cast(rows, src=src, dst=dst):
            dst[rows, :] = src[rows, :].astype(BF16)

        _for_row_chunks(src.shape[0], max(16, CAST_CHUNK_ELEMS // src.shape[1]), cast)

    @pl.when(qi == 0)
    def _():
        kb_ref[...] = k_ref[...].astype(BF16)
        vb_ref[...] = v_ref[...].astype(BF16)

    bias = bias_ref[h]
    carry_ref[...] = jnp.zeros_like(carry_ref)
    acc_ref[...] = jnp.zeros_like(acc_ref)

    def block(r0, kblk, vblk, valid, u):
        z = _dot_nt(q_ref[r0:, :], kblk) + bias
        w, carry = _sb_weights(z, valid, u, carry_ref[r0:, :])
        carry_ref[r0:, :] = carry
        acc_ref[r0:, :] += _dot(w.astype(BF16), vblk)

    for d in reversed(range(tq // tk)):
        r0 = d * tk
        off = pl.multiple_of(qi * tq + r0, tk)
        row = lax.broadcasted_iota(jnp.int32, (tq - r0, tk), 0)
        col = lax.broadcasted_iota(jnp.int32, (tq - r0, tk), 1)
        block(r0, kb_ref[pl.ds(off, tk), :], vb_ref[pl.ds(off, tk), :], col < row, u1_ref[...])

    n_before = qi * (tq // tk)

    def body(it, c):
        off = pl.multiple_of((n_before - 1 - it) * tk, tk)
        block(0, kb_ref[pl.ds(off, tk), :], vb_ref[pl.ds(off, tk), :], None, u1_ref[...])
        return c

    lax.fori_loop(0, n_before, body, 0)

    colm = lax.broadcasted_iota(jnp.int32, (tq, LANES), 1)
    block(0, km_ref[...].astype(BF16), vm_ref[...].astype(BF16), colm < N_META, u1m_ref[...])
    o_ref[...] = acc_ref[...].astype(o_ref.dtype)


def _prompt_attention(q, k, v, k_meta, v_meta, bias, batch, seq, cast_along=(), tq_pref=2048, tk_pref=256):
    m, sbw = q.shape
    nh = sbw // HEAD_DIM
    tq = _tile(seq, tq_pref)
    tk = _tile(tq, tk_pref)
    nq = seq // tq
    pad = ((0, LANES - N_META), (0, 0))
    km = jnp.pad(k_meta, pad)
    vm = jnp.pad(v_meta, pad)
    kern = functools.partial(_prompt_attn_kernel, tq=tq, tk=tk, n_cast=len(cast_along))
    qspec = pl.BlockSpec((tq, HEAD_DIM), lambda b, h, i: (b * nq + i, h))
    kvspec = pl.BlockSpec((seq, HEAD_DIM), lambda b, h, i: (b, h))
    mspec = pl.BlockSpec((LANES, HEAD_DIM), lambda b, h, i: (0, h))
    n_steps = batch * nh * nq
    cast_specs = []
    for w in cast_along:
        rows, cols = w.shape
        slab = rows // n_steps
        assert slab * n_steps == rows and slab % 16 == 0
        cast_specs.append(pl.BlockSpec((slab, cols), lambda b, h, i: ((b * nh + h) * nq + i, 0)))
    out = pl.pallas_call(
        kern,
        grid=(batch, nh, nq),
        in_specs=[pl.BlockSpec(memory_space=pltpu.SMEM), qspec, kvspec, kvspec, mspec, mspec,
                  pl.BlockSpec((2 * tk, tk), lambda b, h, i: (0, 0)),
                  pl.BlockSpec((2 * LANES, LANES), lambda b, h, i: (0, 0))] + cast_specs,
        out_specs=[qspec] + cast_specs,
        out_shape=[jax.ShapeDtypeStruct((m, sbw), BF16)] + [jax.ShapeDtypeStruct(w.shape, BF16) for w in cast_along],
        scratch_shapes=[pltpu.VMEM((seq, HEAD_DIM), BF16), pltpu.VMEM((seq, HEAD_DIM), BF16),
                        pltpu.VMEM((tq, LANES), F32), pltpu.VMEM((tq, HEAD_DIM), F32)],
        compiler_params=_params("parallel", "parallel", "arbitrary"),
        name="prompt_attention",
    )(bias, q, k, v, km, vm, _suffix_matrix(tk, 2), _suffix_matrix(LANES, 2), *cast_along)
    return out[0], tuple(out[1:])


HEAD_TILE = SUBLANES
PAGES_PER_BLOCK = 4


def _sample_attn_kernel(pt_ref, q_ref, kn_ref, vn_ref, bias_ref, u1_ref, ug_ref, mask_ref, sel_ref, selt_ref, *rest,
                        group):
    kc_refs = rest[:group]
    vc_refs = rest[group:2 * group]
    o_ref, carry_ref, acc_ref = rest[2 * group:]
    del pt_ref
    j = pl.program_id(1)
    n_half, hr = q_ref.shape[:2]
    bias = bias_ref[...]

    def attend(k_of, v_of, n_blocks, valid, u):
        r = k_of(0, 0).shape[0]
        m = mask_ref[:, :r]
        halves = []
        for half in range(n_half):
            kcat = jnp.concatenate([k_of(i, half) for i in range(n_blocks)], axis=0)
            zt = _dot_nt(q_ref[half], kcat)
            for i in range(n_blocks):
                halves.extend(_split_bf16(zt[:, i * r:(i + 1) * r] * m))
        zz = _dot(jnp.concatenate(halves, axis=0), sel_ref[:r, :])
        zs = []
        for i in range(n_blocks):
            parts = []
            for half in range(n_half):
                base = (half * n_blocks + i) * 2 * hr
                parts.append(zz[base:base + hr] + zz[base + hr:base + 2 * hr])
            zs.append(jnp.concatenate(parts, axis=0) + bias)
        w, carry = _sb_weights(jnp.concatenate(zs, axis=1), valid, u, carry_ref[...], stack=True)
        carry_ref[...] = carry
        w = w.astype(BF16)
        wrows = jnp.concatenate([w[half * hr:(half + 1) * hr, i * LANES:(i + 1) * LANES]
                                 for half in range(n_half) for i in range(n_blocks)], axis=0)
        wide = _dot(wrows, selt_ref[:, :r])
        for half in range(n_half):
            lhs = jnp.concatenate(
                [(wide[(half * n_blocks + i) * hr:(half * n_blocks + i + 1) * hr] * m).astype(BF16)
                 for i in range(n_blocks)], axis=1)
            vcat = jnp.concatenate([v_of(i, half) for i in range(n_blocks)], axis=0)
            acc_ref[half] += _dot(lhs, vcat)

    @pl.when(j == 0)
    def _():
        carry_ref[...] = jnp.zeros_like(carry_ref)
        acc_ref[...] = jnp.zeros_like(acc_ref)
        t = (lax.broadcasted_iota(jnp.int32, (n_half * hr, LANES), 0) % hr) // HEAD_TILE
        key = lax.broadcasted_iota(jnp.int32, (n_half * hr, LANES), 1)
        attend(lambda i, half: kn_ref[half].astype(BF16), lambda i, half: vn_ref[half].astype(BF16), 1,
               key < t, u1_ref[...])

    def half_page(ref, half):
        rows = ref.shape[0] // n_half
        x = ref[pl.ds(half, rows, stride=n_half), :, :]
        return x.reshape(rows * HEAD_TILE, HEAD_DIM).astype(BF16)

    for first in range(0, group, PAGES_PER_BLOCK):
        last = first + PAGES_PER_BLOCK - 1
        attend(lambda i, half: half_page(kc_refs[last - i], half),
               lambda i, half: half_page(vc_refs[last - i], half), PAGES_PER_BLOCK, None, ug_ref[...])

    @pl.when(j == pl.num_programs(1) - 1)
    def _():
        o_ref[...] = acc_ref[...].astype(o_ref.dtype)


def _sample_attention(q, k_new, v_new, cache_k, cache_v, page_table, bias, group_pref=8):
    db, ds, sbw = q.shape
    nh = sbw // HEAD_DIM
    n_pages = page_table.shape[1]
    n_phys, page = cache_k.shape[:2]
    group = _tile(n_pages, group_pref)
    assert page == LANES and ds <= page and nh % HEAD_TILE == 0 and group % PAGES_PER_BLOCK == 0
    n_half = nh // HEAD_TILE
    hr = ds * HEAD_TILE
    flat = page * HEAD_TILE

    def by_half(a):
        a = a.reshape(db, ds, n_half, HEAD_TILE, HEAD_DIM).transpose(0, 2, 1, 3, 4)
        return a.reshape(db, n_half, hr, HEAD_DIM)

    mask = (np.arange(hr)[:, None] % HEAD_TILE == np.arange(flat)[None, :] % HEAD_TILE).astype(np.float32)
    sel = (np.arange(flat)[:, None] // HEAD_TILE == np.arange(LANES)[None, :]).astype(np.float32)
    bias_rows = jnp.broadcast_to(bias.astype(F32).reshape(n_half, 1, HEAD_TILE, 1),
                                 (n_half, ds, HEAD_TILE, LANES)).reshape(n_half * hr, LANES)
    pt = page_table.reshape(-1).astype(jnp.int32)

    def const(shape):
        return pl.BlockSpec(shape, lambda n, j, pt: (0,) * len(shape))

    seq_spec = pl.BlockSpec((None, n_half, hr, HEAD_DIM), lambda n, j, pt: (n, 0, 0, 0))

    def page_spec(c):
        def imap(n, j, pt):
            return (pt[n * n_pages + (n_pages - 1 - (j * group + c))], 0, 0, 0)
        return pl.BlockSpec((None, page * n_half, HEAD_TILE, HEAD_DIM), imap)

    kern = functools.partial(_sample_attn_kernel, group=group)
    ck = cache_k.reshape(n_phys, page * n_half, HEAD_TILE, HEAD_DIM)
    cv = cache_v.reshape(n_phys, page * n_half, HEAD_TILE, HEAD_DIM)
    out = pl.pallas_call(
        kern,
        grid_spec=pltpu.PrefetchScalarGridSpec(
            num_scalar_prefetch=1,
            grid=(db, n_pages // group),
            in_specs=[seq_spec, seq_spec, seq_spec, const((n_half * hr, LANES)), const((page, page)),
                      const((PAGES_PER_BLOCK * page, PAGES_PER_BLOCK * page)),
                      const((hr, flat)), const((flat, LANES)), const((LANES, flat))]
                     + [page_spec(c) for c in range(group)] * 2,
            out_specs=seq_spec,
            scratch_shapes=[pltpu.VMEM((n_half * hr, LANES), F32), pltpu.VMEM((n_half, hr, HEAD_DIM), F32)]),
        out_shape=jax.ShapeDtypeStruct((db, n_half, hr, HEAD_DIM), BF16),
        compiler_params=_params("parallel", "arbitrary"),
        name="sample_attention",
    )(pt, by_half(q), by_half(k_new), by_half(v_new),
      bias_rows, _suffix_matrix(page), _suffix_matrix(PAGES_PER_BLOCK * page), jnp.asarray(mask), jnp.asarray(sel, BF16),
      jnp.asarray(sel.T, BF16), *([ck] * group), *([cv] * group))
    out = out.reshape(db, n_half, ds, HEAD_TILE, HEAD_DIM).transpose(0, 2, 1, 3, 4)
    return out.reshape(db, ds, sbw)


def _pool_tail(b_ref, ext, w_ref, s_ref, o_ref):
    ng, gd = w_ref.shape[:2]
    for g in range(ng):
        cols = slice(g * gd, (g + 1) * gd)
        hi, lo = _split_bf16(ext[:, cols])
        d = _dot(b_ref[g], jnp.concatenate([hi, lo], axis=0))
        o_ref[:, cols] = (_dot(d.astype(BF16), w_ref[g]) * s_ref[:, cols]).astype(o_ref.dtype)


def _pool_prompt_kernel(p_ref, halo_ref, meta_ref, b_ref, w_ref, s_ref, o_ref, *, tiles_per_seq):
    first = (pl.program_id(0) % tiles_per_seq) == 0
    halo = jnp.where(first, meta_ref[...], halo_ref[...])
    _pool_tail(b_ref, jnp.concatenate([halo, p_ref[...]], axis=0), w_ref, s_ref, o_ref)


def _pool_sample_kernel(e_ref, b_ref, w_ref, s_ref, o_ref):
    _pool_tail(b_ref, e_ref[...], w_ref, s_ref, o_ref)


def _window_rows(n_out, n_in, first):
    b = np.zeros((len(POOL_WINDOWS), n_out, n_in), np.float32)
    for g, w in enumerate(POOL_WINDOWS):
        for r in range(n_out):
            b[g, r, first + r - w + 1:first + r + 1] += 1.0 / w
            b[g, r, first + r] -= 1.0
    return b


def _pool_prompt(p, p_meta, w_pool, pool_scale, seq, tm_pref=512):
    m, pw = p.shape
    ng, gd, _ = w_pool.shape
    tm = _tile(seq, tm_pref)
    assert tm % LANES == 0
    hb = tm // LANES
    meta = jnp.pad(p_meta, ((LANES - N_META, 0), (0, 0)))
    band = np.tile(_window_rows(tm, LANES + tm, LANES), (1, 1, 2))
    kern = functools.partial(_pool_prompt_kernel, tiles_per_seq=seq // tm)
    once = pl.Buffered(1)
    return pl.pallas_call(
        kern,
        grid=(m // tm,),
        in_specs=[pl.BlockSpec((tm, pw), lambda i: (i, 0)),
                  pl.BlockSpec((LANES, pw), lambda i: (jnp.maximum(i * hb - 1, 0), 0)),
                  pl.BlockSpec((LANES, pw), lambda i: (0, 0), pipeline_mode=once),
                  pl.BlockSpec(band.shape, lambda i: (0, 0, 0), pipeline_mode=once),
                  pl.BlockSpec((ng, gd, gd), lambda i: (0, 0, 0), pipeline_mode=once),
                  pl.BlockSpec((1, pw), lambda i: (0, 0))],
        out_specs=pl.BlockSpec((tm, pw), lambda i: (i, 0)),
        out_shape=jax.ShapeDtypeStruct((m, pw), BF16),
        compiler_params=_params("parallel"),
        name="pool_prompt",
    )(p, p, meta, jnp.asarray(band, BF16), w_pool, pool_scale)


_SAMPLE_EXT_ROWS = 32


def _pool_sample(state, p_new, w_pool, pool_scale, seqs_pref=16):
    db, ds, pw = p_new.shape
    ng, gd, _ = w_pool.shape
    ns = _tile(db, seqs_pref)
    assert POOL_BUF + ds <= _SAMPLE_EXT_ROWS and ds <= SUBLANES
    ext = jnp.concatenate(
        [state, p_new, jnp.zeros((db, _SAMPLE_EXT_ROWS - POOL_BUF - ds, pw), F32)], axis=1)
    ext = ext.reshape(db * _SAMPLE_EXT_ROWS, pw)
    one = np.zeros((len(POOL_WINDOWS), SUBLANES, _SAMPLE_EXT_ROWS), np.float32)
    one[:, :ds] = _window_rows(ds, _SAMPLE_EXT_ROWS, POOL_BUF)
    band = np.zeros((len(POOL_WINDOWS), ns * SUBLANES, ns * _SAMPLE_EXT_ROWS), np.float32)
    for s in range(ns):
        band[:, s * SUBLANES:(s + 1) * SUBLANES, s * _SAMPLE_EXT_ROWS:(s + 1) * _SAMPLE_EXT_ROWS] = one
    band = np.tile(band, (1, 1, 2))
    once = pl.Buffered(1)
    out = pl.pallas_call(
        _pool_sample_kernel,
        grid=(db // ns,),
        in_specs=[pl.BlockSpec((ns * _SAMPLE_EXT_ROWS, pw), lambda i: (i, 0)),
                  pl.BlockSpec(band.shape, lambda i: (0, 0, 0), pipeline_mode=once),
                  pl.BlockSpec((ng, gd, gd), lambda i: (0, 0, 0), pipeline_mode=once),
                  pl.BlockSpec((1, pw), lambda i: (0, 0))],
        out_specs=pl.BlockSpec((ns * SUBLANES, pw), lambda i: (i, 0)),
        out_shape=jax.ShapeDtypeStruct((db * SUBLANES, pw), BF16),
        compiler_params=_params("parallel"),
        name="pool_sample",
    )(ext, jnp.asarray(band, BF16), w_pool, pool_scale)
    return out.reshape(db, SUBLANES, pw)[:, :ds]


def _accumulate(o_ref, terms, col_chunk=1024):
    n = o_ref.shape[1]
    cn = min(n, col_chunk)
    for c in range(n // cn):
        cols = slice(c * cn, (c + 1) * cn)
        part = _dot(terms[0][0][...], terms[0][1][:, cols])
        for a_ref, w_ref in terms[1:]:
            part += _dot(a_ref[...], w_ref[:, cols])
        o_ref[:, cols] += part


def _epilogue_rows(step, step_rows, chunk, fn):
    chunk = _tile(step_rows, chunk)

    def body(c, carry):
        local = pl.multiple_of(c * chunk, chunk)
        fn(pl.ds(local, chunk), pl.ds(pl.multiple_of(step * step_rows, step_rows) + local, chunk))
        return carry

    lax.fori_loop(0, step_rows // chunk, body, 0)


def _outproj_kernel(a1_ref, a2_ref, w1_ref, w2_ref, x_ref, gpost_ref, gpre_ref, h_ref, hn_ref, acc_ref, *, nk):
    s = pl.program_id(1)

    @pl.when(s == 0)
    def _():
        acc_ref[...] = jnp.zeros_like(acc_ref)

    @pl.when(s < nk)
    def _():
        _accumulate(acc_ref, [(a1_ref, w1_ref), (a2_ref, w2_ref)])

    @pl.when(s >= nk)
    def _():
        gpost = gpost_ref[...]
        gpre = gpre_ref[...]

        def finish(rows, acc_rows):
            h = x_ref[rows, :] + _rms(acc_ref[acc_rows, :], gpost)
            h_ref[rows, :] = h
            hn_ref[rows, :] = _rms(h, gpre).astype(BF16)

        _epilogue_rows(s - nk, x_ref.shape[0], 32, finish)


def _out_projection(a1, a2, w_out, x, g_post, g_pre, tm_pref=1024, tk_pref=512):
    m, k1 = a1.shape
    d = w_out.shape[1]
    tm = _tile(m, tm_pref)
    tk = _tile(k1, tk_pref)
    nk = k1 // tk
    er = _tile(tm, EPILOGUE_ROWS)
    ne = tm // er

    def kstep(s):
        return jnp.minimum(s, nk - 1)

    def erow(i, s):
        return i * ne + jnp.maximum(s - nk, 0)

    return pl.pallas_call(
        functools.partial(_outproj_kernel, nk=nk),
        grid=(m // tm, nk + ne),
        in_specs=[pl.BlockSpec((tm, tk), lambda i, s: (i, kstep(s))),
                  pl.BlockSpec((tm, tk), lambda i, s: (i, kstep(s))),
                  pl.BlockSpec((tk, d), lambda i, s: (kstep(s), 0)),
                  pl.BlockSpec((tk, d), lambda i, s: (kstep(s) + nk, 0)),
                  pl.BlockSpec((er, d), lambda i, s: (erow(i, s), 0)),
                  pl.BlockSpec((1, d), lambda i, s: (0, 0)),
                  pl.BlockSpec((1, d), lambda i, s: (0, 0))],
        out_specs=[pl.BlockSpec((er, d), lambda i, s: (erow(i, s), 0)),
                   pl.BlockSpec((er, d), lambda i, s: (erow(i, s), 0))],
        out_shape=[jax.ShapeDtypeStruct((m, d), F32), jax.ShapeDtypeStruct((m, d), BF16)],
        scratch_shapes=[pltpu.VMEM((tm, d), F32)],
        compiler_params=_params("parallel", "arbitrary"),
        name="out_projection",
    )(a1, a2, w_out, w_out, x, g_post, g_pre)


def _up_kernel(hn_ref, w_ref, o_ref):
    u = jnp.maximum(_dot(hn_ref[...], w_ref[...]), 0.0)
    o_ref[...] = (u * u).astype(o_ref.dtype)


def _mlp_up(hn, w_up, tm_pref=1024, tn_pref=512):
    m, d = hn.shape
    f = w_up.shape[1]
    tm = _tile(m, tm_pref)
    tn = _tile(f, tn_pref)
    return pl.pallas_call(
        _up_kernel,
        grid=(m // tm, f // tn),
        in_specs=[pl.BlockSpec((tm, d), lambda i, j: (i, 0)),
                  pl.BlockSpec((d, tn), lambda i, j: (0, j))],
        out_specs=pl.BlockSpec((tm, tn), lambda i, j: (i, j)),
        out_shape=jax.ShapeDtypeStruct((m, f), BF16),
        compiler_params=_params("parallel", "arbitrary"),
        name="mlp_up",
    )(hn, w_up)


def _down_kernel(u_ref, w_ref, h_ref, g_ref, y_ref, acc_ref, *, nk):
    s = pl.program_id(1)

    @pl.when(s == 0)
    def _():
        acc_ref[...] = jnp.zeros_like(acc_ref)

    @pl.when(s < nk)
    def _():
        _accumulate(acc_ref, [(u_ref, w_ref)])

    @pl.when(s >= nk)
    def _():
        g = g_ref[...]

        def finish(rows, acc_rows):
            y_ref[rows, :] = h_ref[rows, :] + _rms(acc_ref[acc_rows, :], g)

        _epilogue_rows(s - nk, h_ref.shape[0], 32, finish)


def _mlp_down(u2, w_down, h, g_post, tm_pref=1024, tk_pref=1024):
    m, f = u2.shape
    d = w_down.shape[1]
    tm = _tile(m, tm_pref)
    tk = _tile(f, tk_pref)
    nk = f // tk
    er = _tile(tm, EPILOGUE_ROWS)
    ne = tm // er

    def kstep(s):
        return jnp.minimum(s, nk - 1)

    def erow(i, s):
        return i * ne + jnp.maximum(s - nk, 0)

    return pl.pallas_call(
        functools.partial(_down_kernel, nk=nk),
        grid=(m // tm, nk + ne),
        in_specs=[pl.BlockSpec((tm, tk), lambda i, s: (i, kstep(s))),
                  pl.BlockSpec((tk, d), lambda i, s: (kstep(s), 0)),
                  pl.BlockSpec((er, d), lambda i, s: (erow(i, s), 0)),
                  pl.BlockSpec((1, d), lambda i, s: (0, 0))],
        out_specs=pl.BlockSpec((er, d), lambda i, s: (erow(i, s), 0)),
        out_shape=jax.ShapeDtypeStruct((m, d), F32),
        scratch_shapes=[pltpu.VMEM((tm, d), F32)],
        compiler_params=_params("parallel", "arbitrary"),
        name="mlp_down",
    )(u2, w_down, h, g_post)


def _finish_layer(x, attn, pool, w_out, g_mix_post, g_mlp_pre, w_up, w_down, g_mlp_post):
    h, hn = _out_projection(attn, pool, w_out, x, g_mix_post, g_mlp_pre)
    u2 = _mlp_up(hn, w_up)
    return _mlp_down(u2, w_down, h, g_mlp_post)


def kernel(x_prompt, x_sample, cache_k, cache_v, state_pool, page_table, meta, g_mix_pre, w_in, sb_bias,
           w_pool, pool_scale, w_out, g_mix_post, g_mlp_pre, w_up, w_down, g_mlp_post):
    depth = w_in.shape[0]
    assert depth == 1, "meta rows skip the MLP, which is only valid for a single layer"
    b, s, d = x_prompt.shape
    db, ds, _ = x_sample.shape
    nh = cache_k.shape[3]
    sbw = nh * HEAD_DIM
    pw = w_pool.shape[1] * w_pool.shape[2]
    layer = 0

    w_in_b = w_in[layer].astype(BF16)
    w_pool_b = w_pool[layer].astype(BF16)
    g_pre = g_mix_pre[layer][None, :]
    scale_row = pool_scale[layer][None, :]
    bias = sb_bias[layer].astype(F32) * LOG2E

    xp = x_prompt.reshape(b * s, d)
    q_p, k_p, v_p, p_p = _in_projection(xp, g_pre, w_in_b, sbw, pw)
    xs = x_sample.reshape(db * ds, d)
    n_s = db * ds
    q_sm, k_sm, v_sm, p_sm = _in_projection(jnp.concatenate([xs, meta.astype(F32)], axis=0), g_pre, w_in_b, sbw, pw,
                                            tm_pref=n_s + N_META)
    q_s, k_s, v_s, p_s = q_sm[:n_s], k_sm[:n_s], v_sm[:n_s], p_sm[:n_s]
    k_m, v_m, p_m = k_sm[n_s:], v_sm[n_s:], p_sm[n_s:]

    attn_p, (w_out_b, w_up_b, w_down_b) = _prompt_attention(
        q_p, k_p, v_p, k_m, v_m, bias, b, s, cast_along=(w_out[layer], w_up[layer], w_down[layer]))
    mlp = (w_out_b, g_mix_post[layer][None, :], g_mlp_pre[layer][None, :], w_up_b, w_down_b,
           g_mlp_post[layer][None, :])
    pool_p = _pool_prompt(p_p, p_m, w_pool_b, scale_row, s)
    y_p = _finish_layer(xp, attn_p, pool_p, *mlp)

    attn_s = _sample_attention(q_s.reshape(db, ds, sbw), k_s.reshape(db, ds, sbw), v_s.reshape(db, ds, sbw),
                               cache_k[layer], cache_v[layer], page_table, bias)
    pool_s = _pool_sample(state_pool[layer], p_s.reshape(db, ds, pw), w_pool_b, scale_row)
    y_s = _finish_layer(xs, attn_s.reshape(n_s, sbw), pool_s.reshape(n_s, pw), *mlp)

    def with_meta(a_meta, a):
        full = jnp.concatenate([jnp.broadcast_to(a_meta[None], (b, N_META, sbw)), a.reshape(b, s, sbw)], axis=1)
        return full.reshape(1, b, N_META + s, nh, HEAD_DIM)

    assert s >= POOL_BUF
    pool_prompt = p_p.reshape(b, s, pw)[:, -POOL_BUF:]
    pool_sample = jnp.concatenate([state_pool[layer], p_s.reshape(db, ds, pw)], axis=1)[:, -POOL_BUF:]
    return (y_p.reshape(b, s, d), y_s.reshape(db, ds, d),
            with_meta(k_m, k_p), with_meta(v_m, v_p), pool_prompt[None],
            k_s.reshape(1, db, ds, nh, HEAD_DIM), v_s.reshape(1, db, ds, nh, HEAD_DIM), pool_sample[None])
```

```python
import functools

import numpy as np
import jax
import jax.numpy as jnp
from jax import lax
from jax.experimental import pallas as pl
from jax.experimental.pallas import tpu as pltpu

F32 = jnp.float32
BF16 = jnp.bfloat16

HEAD_DIM = 128
N_META = 16
POOL_WINDOWS = (2, 4, 8, 16)
POOL_BUF = max(POOL_WINDOWS) - 1
EPS = 1e-6
EPILOGUE_ROWS = 128
CAST_CHUNK_ELEMS = 256 * 1024
LOG2E = 1.4426950408889634
LANES = 128
SUBLANES = 8
VMEM_LIMIT_BYTES = 56 * 1024 * 1024


def _tile(n, pref):
    if n <= pref:
        return n
    t = pref
    while n % t:
        t //= 2
    return t


def _params(*sem):
    return pltpu.CompilerParams(dimension_semantics=sem, vmem_limit_bytes=VMEM_LIMIT_BYTES)


def _dot(a, b):
    return jnp.dot(a, b, preferred_element_type=F32)


def _dot_nt(a, b):
    return lax.dot_general(a, b, (((1,), (1,)), ((), ())), preferred_element_type=F32)


def _split_bf16(x):
    hi = x.astype(BF16)
    lo = (x - hi.astype(F32)).astype(BF16)
    return hi, lo


def _rms(x, g):
    ms = jnp.mean(x * x, axis=-1, keepdims=True)
    return x * lax.rsqrt(ms + EPS) * g


def _for_row_chunks(n_rows, chunk, fn):
    chunk = _tile(n_rows, chunk)

    def body(c, carry):
        fn(pl.ds(pl.multiple_of(c * chunk, chunk), chunk))
        return carry

    lax.fori_loop(0, n_rows // chunk, body, 0)


def _inproj_kernel(x_ref, g_ref, wq_ref, wk_ref, wv_ref, wp_ref, q_ref, k_ref, v_ref, p_ref, xn_ref, *, q_scale):
    @pl.when(pl.program_id(1) == 0)
    def _():
        g = g_ref[...]

        def norm(rows):
            xn_ref[rows, :] = _rms(x_ref[rows, :], g).astype(BF16)

        _for_row_chunks(x_ref.shape[0], 64, norm)

    xn = xn_ref[...]
    q_ref[...] = (_dot(xn, wq_ref[...]) * q_scale).astype(q_ref.dtype)
    k_ref[...] = _dot(xn, wk_ref[...])
    v_ref[...] = _dot(xn, wv_ref[...])
    p_ref[...] = _dot(xn, wp_ref[...])


def _in_projection(x, g, w_in, sbw, pw, tm_pref=512, tn_pref=256):
    m, d = x.shape
    tm = _tile(m, tm_pref)
    tn = _tile(min(sbw, pw), tn_pref)
    assert sbw % tn == 0 and pw % tn == 0 and sbw == pw
    nj = sbw // tn

    def wspec(c):
        return pl.BlockSpec((d, tn), lambda i, j: (0, j + c * nj))

    ospec = pl.BlockSpec((tm, tn), lambda i, j: (i, j))
    return pl.pallas_call(
        functools.partial(_inproj_kernel, q_scale=HEAD_DIM ** -0.5 * LOG2E),
        grid=(m // tm, nj),
        in_specs=[pl.BlockSpec((tm, d), lambda i, j: (i, 0)),
                  pl.BlockSpec((1, d), lambda i, j: (0, 0)),
                  wspec(0), wspec(1), wspec(2), wspec(3)],
        out_specs=[ospec, ospec, ospec, ospec],
        out_shape=[jax.ShapeDtypeStruct((m, sbw), BF16),
                   jax.ShapeDtypeStruct((m, sbw), F32),
                   jax.ShapeDtypeStruct((m, sbw), F32),
                   jax.ShapeDtypeStruct((m, pw), F32)],
        scratch_shapes=[pltpu.VMEM((tm, d), BF16)],
        compiler_params=_params("parallel", "arbitrary"),
        name="in_projection",
    )(x, g, w_in, w_in, w_in, w_in)


def _suffix_matrix(tk, copies=1):
    j = np.arange(tk)[:, None]
    s = np.arange(tk)[None, :]
    return jnp.asarray(np.tile((j > s).astype(np.float32), (copies, 1)), BF16)


def _sb_weights(z, valid, u, carry, stack=False):
    r, tk = z.shape
    sp = jnp.maximum(z, 0.0) + jnp.log2(1.0 + jnp.exp2(-jnp.abs(z)))
    if valid is not None:
        sp = jnp.where(valid, sp, 0.0)
    hi, lo = _split_bf16(sp)
    if stack:
        su = _dot(jnp.concatenate([hi, lo], axis=0), u)
        su = su[:r] + su[r:]
    else:
        su = _dot(jnp.concatenate([hi, lo], axis=1), u)
    later = su + jnp.concatenate([carry] * (tk // LANES), axis=1)
    w = jnp.exp2(z - sp - later)
    if valid is not None:
        w = jnp.where(valid, w, 0.0)
    total = jnp.broadcast_to(jnp.sum(sp, axis=1, keepdims=True), (r, LANES))
    return w, carry + total


def _prompt_attn_kernel(bias_ref, q_ref, k_ref, v_ref, km_ref, vm_ref, u1_ref, u1m_ref, *rest, tq, tk, n_cast):
    cast_in = rest[:n_cast]
    o_ref = rest[n_cast]
    cast_out = rest[n_cast + 1:2 * n_cast + 1]
    kb_ref, vb_ref, carry_ref, acc_ref = rest[2 * n_cast + 1:]
    h = pl.program_id(1)
    qi = pl.program_id(2)

    for src, dst in zip(cast_in, cast_out):
        def cast(rows, src=src, dst=dst):
            dst[rows, :] = src[rows, :].astype(BF16)

        _for_row_chunks(src.shape[0], max(16, CAST_CHUNK_ELEMS // src.shape[1]), cast)

    @pl.when(qi == 0)
    def _():
        kb_ref[...] = k_ref[...].astype(BF16)
        vb_ref[...] = v_ref[...].astype(BF16)

    bias = bias_ref[h]
    carry_ref[...] = jnp.zeros_like(carry_ref)
    acc_ref[...] = jnp.zeros_like(acc_ref)

    def block(r0, kblk, vblk, valid, u):
        z = _dot_nt(q_ref[r0:, :], kblk) + bias
        w, carry = _sb_weights(z, valid, u, carry_ref[r0:, :])
        carry_ref[r0:, :] = carry
        acc_ref[r0:, :] += _dot(w.astype(BF16), vblk)

    for d in reversed(range(tq // tk)):
        r0 = d * tk
        off = pl.multiple_of(qi * tq + r0, tk)
        row = lax.broadcasted_iota(jnp.int32, (tq - r0, tk), 0)
        col = lax.broadcasted_iota(jnp.int32, (tq - r0, tk), 1)
        block(r0, kb_ref[pl.ds(off, tk), :], vb_ref[pl.ds(off, tk), :], col < row, u1_ref[...])

    n_before = qi * (tq // tk)

    def body(it, c):
        off = pl.multiple_of((n_before - 1 - it) * tk, tk)
        block(0, kb_ref[pl.ds(off, tk), :], vb_ref[pl.ds(off, tk), :], None, u1_ref[...])
        return c

    lax.fori_loop(0, n_before, body, 0)

    colm = lax.broadcasted_iota(jnp.int32, (tq, LANES), 1)
    block(0, km_ref[...].astype(BF16), vm_ref[...].astype(BF16), colm < N_META, u1m_ref[...])
    o_ref[...] = acc_ref[...].astype(o_ref.dtype)


def _prompt_attention(q, k, v, k_meta, v_meta, bias, batch, seq, cast_along=(), tq_pref=2048, tk_pref=256):
    m, sbw = q.shape
    nh = sbw // HEAD_DIM
    tq = _tile(seq, tq_pref)
    tk = _tile(tq, tk_pref)
    nq = seq // tq
    pad = ((0, LANES - N_META), (0, 0))
    km = jnp.pad(k_meta, pad)
    vm = jnp.pad(v_meta, pad)
    kern = functools.partial(_prompt_attn_kernel, tq=tq, tk=tk, n_cast=len(cast_along))
    qspec = pl.BlockSpec((tq, HEAD_DIM), lambda b, h, i: (b * nq + i, h))
    kvspec = pl.BlockSpec((seq, HEAD_DIM), lambda b, h, i: (b, h))
    mspec = pl.BlockSpec((LANES, HEAD_DIM), lambda b, h, i: (0, h))
    n_steps = batch * nh * nq
    cast_specs = []
    for w in cast_along:
        rows, cols = w.shape
        slab = rows // n_steps
        assert slab * n_steps == rows and slab % 16 == 0
        cast_specs.append(pl.BlockSpec((slab, cols), lambda b, h, i: ((b * nh + h) * nq + i, 0)))
    out = pl.pallas_call(
        kern,
        grid=(batch, nh, nq),
        in_specs=[pl.BlockSpec(memory_space=pltpu.SMEM), qspec, kvspec, kvspec, mspec, mspec,
                  pl.BlockSpec((2 * tk, tk), lambda b, h, i: (0, 0)),
                  pl.BlockSpec((2 * LANES, LANES), lambda b, h, i: (0, 0))] + cast_specs,
        out_specs=[qspec] + cast_specs,
        out_shape=[jax.ShapeDtypeStruct((m, sbw), BF16)] + [jax.ShapeDtypeStruct(w.shape, BF16) for w in cast_along],
        scratch_shapes=[pltpu.VMEM((seq, HEAD_DIM), BF16), pltpu.VMEM((seq, HEAD_DIM), BF16),
                        pltpu.VMEM((tq, LANES), F32), pltpu.VMEM((tq, HEAD_DIM), F32)],
        compiler_params=_params("parallel", "parallel", "arbitrary"),
        name="prompt_attention",
    )(bias, q, k, v, km, vm, _suffix_matrix(tk, 2), _suffix_matrix(LANES, 2), *cast_along)
    return out[0], tuple(out[1:])


HEAD_TILE = SUBLANES
PAGES_PER_BLOCK = 4


def _sample_attn_kernel(pt_ref, q_ref, kn_ref, vn_ref, bias_ref, u1_ref, ug_ref, mask_ref, sel_ref, selt_ref, *rest,
                        group):
    kc_refs = rest[:group]
    vc_refs = rest[group:2 * group]
    o_ref, carry_ref, acc_ref = rest[2 * group:]
    del pt_ref
    j = pl.program_id(1)
    n_half, hr = q_ref.shape[:2]
    bias = bias_ref[...]

    def attend(k_of, v_of, n_blocks, valid, u):
        r = k_of(0, 0).shape[0]
        m = mask_ref[:, :r]
        halves = []
        for half in range(n_half):
            kcat = jnp.concatenate([k_of(i, half) for i in range(n_blocks)], axis=0)
            zt = _dot_nt(q_ref[half], kcat)
            for i in range(n_blocks):
                halves.extend(_split_bf16(zt[:, i * r:(i + 1) * r] * m))
        zz = _dot(jnp.concatenate(halves, axis=0), sel_ref[:r, :])
        zs = []
        for i in range(n_blocks):
            parts = []
            for half in range(n_half):
                base = (half * n_blocks + i) * 2 * hr
                parts.append(zz[base:base + hr] + zz[base + hr:base + 2 * hr])
            zs.append(jnp.concatenate(parts, axis=0) + bias)
        w, carry = _sb_weights(jnp.concatenate(zs, axis=1), valid, u, carry_ref[...], stack=True)
        carry_ref[...] = carry
        w = w.astype(BF16)
        wrows = jnp.concatenate([w[half * hr:(half + 1) * hr, i * LANES:(i + 1) * LANES]
                                 for half in range(n_half) for i in range(n_blocks)], axis=0)
        wide = _dot(wrows, selt_ref[:, :r])
        for half in range(n_half):
            lhs = jnp.concatenate(
                [(wide[(half * n_blocks + i) * hr:(half * n_blocks + i + 1) * hr] * m).astype(BF16)
                 for i in range(n_blocks)], axis=1)
            vcat = jnp.concatenate([v_of(i, half) for i in range(n_blocks)], axis=0)
            acc_ref[half] += _dot(lhs, vcat)

    @pl.when(j == 0)
    def _():
        carry_ref[...] = jnp.zeros_like(carry_ref)
        acc_ref[...] = jnp.zeros_like(acc_ref)
        t = (lax.broadcasted_iota(jnp.int32, (n_half * hr, LANES), 0) % hr) // HEAD_TILE
        key = lax.broadcasted_iota(jnp.int32, (n_half * hr, LANES), 1)
        attend(lambda i, half: kn_ref[half].astype(BF16), lambda i, half: vn_ref[half].astype(BF16), 1,
               key < t, u1_ref[...])

    def half_page(ref, half):
        rows = ref.shape[0] // n_half
        x = ref[pl.ds(half, rows, stride=n_half), :, :]
        return x.reshape(rows * HEAD_TILE, HEAD_DIM).astype(BF16)

    for first in range(0, group, PAGES_PER_BLOCK):
        last = first + PAGES_PER_BLOCK - 1
        attend(lambda i, half: half_page(kc_refs[last - i], half),
               lambda i, half: half_page(vc_refs[last - i], half), PAGES_PER_BLOCK, None, ug_ref[...])

    @pl.when(j == pl.num_programs(1) - 1)
    def _():
        o_ref[...] = acc_ref[...].astype(o_ref.dtype)


def _sample_attention(q, k_new, v_new, cache_k, cache_v, page_table, bias, group_pref=8):
    db, ds, sbw = q.shape
    nh = sbw // HEAD_DIM
    n_pages = page_table.shape[1]
    n_phys, page = cache_k.shape[:2]
    group = _tile(n_pages, group_pref)
    assert page == LANES and ds <= page and nh % HEAD_TILE == 0 and group % PAGES_PER_BLOCK == 0
    n_half = nh // HEAD_TILE
    hr = ds * HEAD_TILE
    flat = page * HEAD_TILE

    def by_half(a):
        a = a.reshape(db, ds, n_half, HEAD_TILE, HEAD_DIM).transpose(0, 2, 1, 3, 4)
        return a.reshape(db, n_half, hr, HEAD_DIM)

    mask = (np.arange(hr)[:, None] % HEAD_TILE == np.arange(flat)[None, :] % HEAD_TILE).astype(np.float32)
    sel = (np.arange(flat)[:, None] // HEAD_TILE == np.arange(LANES)[None, :]).astype(np.float32)
    bias_rows = jnp.broadcast_to(bias.astype(F32).reshape(n_half, 1, HEAD_TILE, 1),
                                 (n_half, ds, HEAD_TILE, LANES)).reshape(n_half * hr, LANES)
    pt = page_table.reshape(-1).astype(jnp.int32)

    def const(shape):
        return pl.BlockSpec(shape, lambda n, j, pt: (0,) * len(shape))

    seq_spec = pl.BlockSpec((None, n_half, hr, HEAD_DIM), lambda n, j, pt: (n, 0, 0, 0))

    def page_spec(c):
        def imap(n, j, pt):
            return (pt[n * n_pages + (n_pages - 1 - (j * group + c))], 0, 0, 0)
        return pl.BlockSpec((None, page * n_half, HEAD_TILE, HEAD_DIM), imap)

    kern = functools.partial(_sample_attn_kernel, group=group)
    ck = cache_k.reshape(n_phys, page * n_half, HEAD_TILE, HEAD_DIM)
    cv = cache_v.reshape(n_phys, page * n_half, HEAD_TILE, HEAD_DIM)
    out = pl.pallas_call(
        kern,
        grid_spec=pltpu.PrefetchScalarGridSpec(
            num_scalar_prefetch=1,
            grid=(db, n_pages // group),
            in_specs=[seq_spec, seq_spec, seq_spec, const((n_half * hr, LANES)), const((page, page)),
                      const((PAGES_PER_BLOCK * page, PAGES_PER_BLOCK * page)),
                      const((hr, flat)), const((flat, LANES)), const((LANES, flat))]
                     + [page_spec(c) for c in range(group)] * 2,
            out_specs=seq_spec,
            scratch_shapes=[pltpu.VMEM((n_half * hr, LANES), F32), pltpu.VMEM((n_half, hr, HEAD_DIM), F32)]),
        out_shape=jax.ShapeDtypeStruct((db, n_half, hr, HEAD_DIM), BF16),
        compiler_params=_params("parallel", "arbitrary"),
        name="sample_attention",
    )(pt, by_half(q), by_half(k_new), by_half(v_new),
      bias_rows, _suffix_matrix(page), _suffix_matrix(PAGES_PER_BLOCK * page), jnp.asarray(mask), jnp.asarray(sel, BF16),
      jnp.asarray(sel.T, BF16), *([ck] * group), *([cv] * group))
    out = out.reshape(db, n_half, ds, HEAD_TILE, HEAD_DIM).transpose(0, 2, 1, 3, 4)
    return out.reshape(db, ds, sbw)


def _pool_tail(b_ref, ext, w_ref, s_ref, o_ref):
    ng, gd = w_ref.shape[:2]
    for g in range(ng):
        cols = slice(g * gd, (g + 1) * gd)
        hi, lo = _split_bf16(ext[:, cols])
        d = _dot(b_ref[g], jnp.concatenate([hi, lo], axis=0))
        o_ref[:, cols] = (_dot(d.astype(BF16), w_ref[g]) * s_ref[:, cols]).astype(o_ref.dtype)


def _pool_prompt_kernel(p_ref, halo_ref, meta_ref, b_ref, w_ref, s_ref, o_ref, *, tiles_per_seq):
    first = (pl.program_id(0) % tiles_per_seq) == 0
    halo = jnp.where(first, meta_ref[...], halo_ref[...])
    _pool_tail(b_ref, jnp.concatenate([halo, p_ref[...]], axis=0), w_ref, s_ref, o_ref)


def _pool_sample_kernel(e_ref, b_ref, w_ref, s_ref, o_ref):
    _pool_tail(b_ref, e_ref[...], w_ref, s_ref, o_ref)


def _window_rows(n_out, n_in, first):
    b = np.zeros((len(POOL_WINDOWS), n_out, n_in), np.float32)
    for g, w in enumerate(POOL_WINDOWS):
        for r in range(n_out):
            b[g, r, first + r - w + 1:first + r + 1] += 1.0 / w
            b[g, r, first + r] -= 1.0
    return b


def _pool_prompt(p, p_meta, w_pool, pool_scale, seq, tm_pref=512):
    m, pw = p.shape
    ng, gd, _ = w_pool.shape
    tm = _tile(seq, tm_pref)
    assert tm % LANES == 0
    hb = tm // LANES
    meta = jnp.pad(p_meta, ((LANES - N_META, 0), (0, 0)))
    band = np.tile(_window_rows(tm, LANES + tm, LANES), (1, 1, 2))
    kern = functools.partial(_pool_prompt_kernel, tiles_per_seq=seq // tm)
    once = pl.Buffered(1)
    return pl.pallas_call(
        kern,
        grid=(m // tm,),
        in_specs=[pl.BlockSpec((tm, pw), lambda i: (i, 0)),
                  pl.BlockSpec((LANES, pw), lambda i: (jnp.maximum(i * hb - 1, 0), 0)),
                  pl.BlockSpec((LANES, pw), lambda i: (0, 0), pipeline_mode=once),
                  pl.BlockSpec(band.shape, lambda i: (0, 0, 0), pipeline_mode=once),
                  pl.BlockSpec((ng, gd, gd), lambda i: (0, 0, 0), pipeline_mode=once),
                  pl.BlockSpec((1, pw), lambda i: (0, 0))],
        out_specs=pl.BlockSpec((tm, pw), lambda i: (i, 0)),
        out_shape=jax.ShapeDtypeStruct((m, pw), BF16),
        compiler_params=_params("parallel"),
        name="pool_prompt",
    )(p, p, meta, jnp.asarray(band, BF16), w_pool, pool_scale)


_SAMPLE_EXT_ROWS = 32


def _pool_sample(state, p_new, w_pool, pool_scale, seqs_pref=16):
    db, ds, pw = p_new.shape
    ng, gd, _ = w_pool.shape
    ns = _tile(db, seqs_pref)
    assert POOL_BUF + ds <= _SAMPLE_EXT_ROWS and ds <= SUBLANES
    ext = jnp.concatenate(
        [state, p_new, jnp.zeros((db, _SAMPLE_EXT_ROWS - POOL_BUF - ds, pw), F32)], axis=1)
    ext = ext.reshape(db * _SAMPLE_EXT_ROWS, pw)
    one = np.zeros((len(POOL_WINDOWS), SUBLANES, _SAMPLE_EXT_ROWS), np.float32)
    one[:, :ds] = _window_rows(ds, _SAMPLE_EXT_ROWS, POOL_BUF)
    band = np.zeros((len(POOL_WINDOWS), ns * SUBLANES, ns * _SAMPLE_EXT_ROWS), np.float32)
    for s in range(ns):
        band[:, s * SUBLANES:(s + 1) * SUBLANES, s * _SAMPLE_EXT_ROWS:(s + 1) * _SAMPLE_EXT_ROWS] = one
    band = np.tile(band, (1, 1, 2))
    once = pl.Buffered(1)
    out = pl.pallas_call(
        _pool_sample_kernel,
        grid=(db // ns,),
        in_specs=[pl.BlockSpec((ns * _SAMPLE_EXT_ROWS, pw), lambda i: (i, 0)),
                  pl.BlockSpec(band.shape, lambda i: (0, 0, 0), pipeline_mode=once),
                  pl.BlockSpec((ng, gd, gd), lambda i: (0, 0, 0), pipeline_mode=once),
                  pl.BlockSpec((1, pw), lambda i: (0, 0))],
        out_specs=pl.BlockSpec((ns * SUBLANES, pw), lambda i: (i, 0)),
        out_shape=jax.ShapeDtypeStruct((db * SUBLANES, pw), BF16),
        compiler_params=_params("parallel"),
        name="pool_sample",
    )(ext, jnp.asarray(band, BF16), w_pool, pool_scale)
    return out.reshape(db, SUBLANES, pw)[:, :ds]


def _accumulate(o_ref, terms, assign=False, col_chunk=1024):
    n = o_ref.shape[1]
    cn = min(n, col_chunk)
    for c in range(n // cn):
        cols = slice(c * cn, (c + 1) * cn)
        part = _dot(terms[0][0][...], terms[0][1][:, cols])
        for a_ref, w_ref in terms[1:]:
            part += _dot(a_ref[...], w_ref[:, cols])
        if assign:
            o_ref[:, cols] = part
        else:
            o_ref[:, cols] += part


def _epilogue_rows(step, step_rows, chunk, fn):
    chunk = _tile(step_rows, chunk)

    def body(c, carry):
        local = pl.multiple_of(c * chunk, chunk)
        fn(pl.ds(local, chunk), pl.ds(pl.multiple_of(step * step_rows, step_rows) + local, chunk))
        return carry

    lax.fori_loop(0, step_rows // chunk, body, 0)


def _outproj_kernel(a1_ref, a2_ref, w1_ref, w2_ref, x_ref, gpost_ref, gpre_ref, h_ref, hn_ref, acc_ref, *, nk):
    s = pl.program_id(1)

    @pl.when(s == 0)
    def _():
        _accumulate(acc_ref, [(a1_ref, w1_ref), (a2_ref, w2_ref)], assign=True)

    @pl.when(jnp.logical_and(s > 0, s < nk))
    def _():
        _accumulate(acc_ref, [(a1_ref, w1_ref), (a2_ref, w2_ref)])

    @pl.when(s >= nk)
    def _():
        gpost = gpost_ref[...]
        gpre = gpre_ref[...]

        def finish(rows, acc_rows):
            h = x_ref[rows, :] + _rms(acc_ref[acc_rows, :], gpost)
            h_ref[rows, :] = h
            hn_ref[rows, :] = _rms(h, gpre).astype(BF16)

        _epilogue_rows(s - nk, x_ref.shape[0], 64, finish)


def _out_projection(a1, a2, w_out, x, g_post, g_pre, tm_pref=1024, tk_pref=512):
    m, k1 = a1.shape
    d = w_out.shape[1]
    tm = _tile(m, tm_pref)
    tk = _tile(k1, tk_pref)
    nk = k1 // tk
    er = _tile(tm, EPILOGUE_ROWS)
    ne = tm // er

    def kstep(s):
        return jnp.minimum(s, nk - 1)

    def erow(i, s):
        return i * ne + jnp.maximum(s - nk, 0)

    return pl.pallas_call(
        functools.partial(_outproj_kernel, nk=nk),
        grid=(m // tm, nk + ne),
        in_specs=[pl.BlockSpec((tm, tk), lambda i, s: (i, kstep(s))),
                  pl.BlockSpec((tm, tk), lambda i, s: (i, kstep(s))),
                  pl.BlockSpec((tk, d), lambda i, s: (kstep(s), 0)),
                  pl.BlockSpec((tk, d), lambda i, s: (kstep(s) + nk, 0)),
                  pl.BlockSpec((er, d), lambda i, s: (erow(i, s), 0)),
                  pl.BlockSpec((1, d), lambda i, s: (0, 0)),
                  pl.BlockSpec((1, d), lambda i, s: (0, 0))],
        out_specs=[pl.BlockSpec((er, d), lambda i, s: (erow(i, s), 0)),
                   pl.BlockSpec((er, d), lambda i, s: (erow(i, s), 0))],
        out_shape=[jax.ShapeDtypeStruct((m, d), F32), jax.ShapeDtypeStruct((m, d), BF16)],
        scratch_shapes=[pltpu.VMEM((tm, d), F32)],
        compiler_params=_params("parallel", "arbitrary"),
        name="out_projection",
    )(a1, a2, w_out, w_out, x, g_post, g_pre)


def _up_kernel(hn_ref, w_ref, o_ref):
    u = jnp.maximum(_dot(hn_ref[...], w_ref[...]), 0.0)
    o_ref[...] = (u * u).astype(o_ref.dtype)


def _mlp_up(hn, w_up, tm_pref=1024, tn_pref=512):
    m, d = hn.shape
    f = w_up.shape[1]
    tm = _tile(m, tm_pref)
    tn = _tile(f, tn_pref)
    return pl.pallas_call(
        _up_kernel,
        grid=(m // tm, f // tn),
        in_specs=[pl.BlockSpec((tm, d), lambda i, j: (i, 0)),
                  pl.BlockSpec((d, tn), lambda i, j: (0, j))],
        out_specs=pl.BlockSpec((tm, tn), lambda i, j: (i, j)),
        out_shape=jax.ShapeDtypeStruct((m, f), BF16),
        compiler_params=_params("parallel", "arbitrary"),
        name="mlp_up",
    )(hn, w_up)


def _down_kernel(u_ref, w_ref, h_ref, g_ref, y_ref, acc_ref, *, nk):
    s = pl.program_id(1)

    @pl.when(s == 0)
    def _():
        _accumulate(acc_ref, [(u_ref, w_ref)], assign=True)

    @pl.when(jnp.logical_and(s > 0, s < nk))
    def _():
        _accumulate(acc_ref, [(u_ref, w_ref)])

    @pl.when(s >= nk)
    def _():
        g = g_ref[...]

        def finish(rows, acc_rows):
            y_ref[rows, :] = h_ref[rows, :] + _rms(acc_ref[acc_rows, :], g)

        _epilogue_rows(s - nk, h_ref.shape[0], 64, finish)


def _mlp_down(u2, w_down, h, g_post, tm_pref=1024, tk_pref=1024):
    m, f = u2.shape
    d = w_down.shape[1]
    tm = _tile(m, tm_pref)
    tk = _tile(f, tk_pref)
    nk = f // tk
    er = _tile(tm, EPILOGUE_ROWS)
    ne = tm // er

    def kstep(s):
        return jnp.minimum(s, nk - 1)

    def erow(i, s):
        return i * ne + jnp.maximum(s - nk, 0)

    return pl.pallas_call(
        functools.partial(_down_kernel, nk=nk),
        grid=(m // tm, nk + ne),
        in_specs=[pl.BlockSpec((tm, tk), lambda i, s: (i, kstep(s))),
                  pl.BlockSpec((tk, d), lambda i, s: (kstep(s), 0)),
                  pl.BlockSpec((er, d), lambda i, s: (erow(i, s), 0)),
                  pl.BlockSpec((1, d), lambda i, s: (0, 0))],
        out_specs=pl.BlockSpec((er, d), lambda i, s: (erow(i, s), 0)),
        out_shape=jax.ShapeDtypeStruct((m, d), F32),
        scratch_shapes=[pltpu.VMEM((tm, d), F32)],
        compiler_params=_params("parallel", "arbitrary"),
        name="mlp_down",
    )(u2, w_down, h, g_post)


def _finish_layer(x, attn, pool, w_out, g_mix_post, g_mlp_pre, w_up, w_down, g_mlp_post):
    h, hn = _out_projection(attn, pool, w_out, x, g_mix_post, g_mlp_pre)
    u2 = _mlp_up(hn, w_up)
    return _mlp_down(u2, w_down, h, g_mlp_post)


def kernel(x_prompt, x_sample, cache_k, cache_v, state_pool, page_table, meta, g_mix_pre, w_in, sb_bias,
           w_pool, pool_scale, w_out, g_mix_post, g_mlp_pre, w_up, w_down, g_mlp_post):
    depth = w_in.shape[0]
    assert depth == 1, "meta rows skip the MLP, which is only valid for a single layer"
    b, s, d = x_prompt.shape
    db, ds, _ = x_sample.shape
    nh = cache_k.shape[3]
    sbw = nh * HEAD_DIM
    pw = w_pool.shape[1] * w_pool.shape[2]
    layer = 0

    w_in_b = w_in[layer].astype(BF16)
    w_pool_b = w_pool[layer].astype(BF16)
    g_pre = g_mix_pre[layer][None, :]
    scale_row = pool_scale[layer][None, :]
    bias = sb_bias[layer].astype(F32) * LOG2E

    xp = x_prompt.reshape(b * s, d)
    q_p, k_p, v_p, p_p = _in_projection(xp, g_pre, w_in_b, sbw, pw)
    xs = x_sample.reshape(db * ds, d)
    n_s = db * ds
    q_sm, k_sm, v_sm, p_sm = _in_projection(jnp.concatenate([xs, meta.astype(F32)], axis=0), g_pre, w_in_b, sbw, pw,
                                            tm_pref=n_s + N_META)
    q_s, k_s, v_s, p_s = q_sm[:n_s], k_sm[:n_s], v_sm[:n_s], p_sm[:n_s]
    k_m, v_m, p_m = k_sm[n_s:], v_sm[n_s:], p_sm[n_s:]

    attn_p, (w_out_b, w_up_b, w_down_b) = _prompt_attention(
        q_p, k_p, v_p, k_m, v_m, bias, b, s, cast_along=(w_out[layer], w_up[layer], w_down[layer]))
    mlp = (w_out_b, g_mix_post[layer][None, :], g_mlp_pre[layer][None, :], w_up_b, w_down_b,
           g_mlp_post[layer][None, :])
    pool_p = _pool_prompt(p_p, p_m, w_pool_b, scale_row, s)
    y_p = _finish_layer(xp, attn_p, pool_p, *mlp)

    attn_s = _sample_attention(q_s.reshape(db, ds, sbw), k_s.reshape(db, ds, sbw), v_s.reshape(db, ds, sbw),
                               cache_k[layer], cache_v[layer], page_table, bias)
    pool_s = _pool_sample(state_pool[layer], p_s.reshape(db, ds, pw), w_pool_b, scale_row)
    y_s = _finish_layer(xs, attn_s.reshape(n_s, sbw), pool_s.reshape(n_s, pw), *mlp)

    def with_meta(a_meta, a):
        full = jnp.concatenate([jnp.broadcast_to(a_meta[None], (b, N_META, sbw)), a.reshape(b, s, sbw)], axis=1)
        return full.reshape(1, b, N_META + s, nh, HEAD_DIM)

    assert s >= POOL_BUF
    pool_prompt = p_p.reshape(b, s, pw)[:, -POOL_BUF:]
    pool_sample = jnp.concatenate([state_pool[layer], p_s.reshape(db, ds, pw)], axis=1)[:, -POOL_BUF:]
    return (y_p.reshape(b, s, d), y_s.reshape(db, ds, d),
            with_meta(k_m, k_p), with_meta(v_m, v_p), pool_prompt[None],
            k_s.reshape(1, db, ds, nh, HEAD_DIM), v_s.reshape(1, db, ds, nh, HEAD_DIM), pool_sample[None])
```

```python
import functools

import numpy as np
import jax
import jax.numpy as jnp
from jax import lax
from jax.experimental import pallas as pl
from jax.experimental.pallas import tpu as pltpu

F32 = jnp.float32
BF16 = jnp.bfloat16

HEAD_DIM = 128
N_META = 16
POOL_WINDOWS = (2, 4, 8, 16)
POOL_BUF = max(POOL_WINDOWS) - 1
EPS = 1e-6
EPILOGUE_ROWS = 128
CAST_CHUNK_ELEMS = 256 * 1024
LOG2E = 1.4426950408889634
LANES = 128
SUBLANES = 8
VMEM_LIMIT_BYTES = 56 * 1024 * 1024


def _tile(n, pref):
    if n <= pref:
        return n
    t = pref
    while n % t:
        t //= 2
    return t


def _params(*sem):
    return pltpu.CompilerParams(dimension_semantics=sem, vmem_limit_bytes=VMEM_LIMIT_BYTES)


def _dot(a, b):
    return jnp.dot(a, b, preferred_element_type=F32)


def _dot_nt(a, b):
    return lax.dot_general(a, b, (((1,), (1,)), ((), ())), preferred_element_type=F32)


def _split_bf16(x):
    hi = x.astype(BF16)
    lo = (x - hi.astype(F32)).astype(BF16)
    return hi, lo


def _rms(x, g):
    ms = jnp.mean(x * x, axis=-1, keepdims=True)
    return x * lax.rsqrt(ms + EPS) * g


def _for_row_chunks(n_rows, chunk, fn):
    chunk = _tile(n_rows, chunk)

    def body(c, carry):
        fn(pl.ds(pl.multiple_of(c * chunk, chunk), chunk))
        return carry

    lax.fori_loop(0, n_rows // chunk, body, 0)


def _inproj_kernel(x_ref, g_ref, wq_ref, wk_ref, wv_ref, wp_ref, q_ref, k_ref, v_ref, p_ref, xn_ref, *, q_scale):
    @pl.when(pl.program_id(1) == 0)
    def _():
        g = g_ref[...]

        def norm(rows):
            xn_ref[rows, :] = _rms(x_ref[rows, :], g).astype(BF16)

        _for_row_chunks(x_ref.shape[0], 64, norm)

    xn = xn_ref[...]
    q_ref[...] = (_dot(xn, wq_ref[...]) * q_scale).astype(q_ref.dtype)
    k_ref[...] = _dot(xn, wk_ref[...])
    v_ref[...] = _dot(xn, wv_ref[...])
    p_ref[...] = _dot(xn, wp_ref[...])


def _in_projection(x, g, w_in, sbw, pw, tm_pref=512, tn_pref=256):
    m, d = x.shape
    tm = _tile(m, tm_pref)
    tn = _tile(min(sbw, pw), tn_pref)
    assert sbw % tn == 0 and pw % tn == 0 and sbw == pw
    nj = sbw // tn

    def wspec(c):
        return pl.BlockSpec((d, tn), lambda i, j: (0, j + c * nj))

    ospec = pl.BlockSpec((tm, tn), lambda i, j: (i, j))
    return pl.pallas_call(
        functools.partial(_inproj_kernel, q_scale=HEAD_DIM ** -0.5 * LOG2E),
        grid=(m // tm, nj),
        in_specs=[pl.BlockSpec((tm, d), lambda i, j: (i, 0)),
                  pl.BlockSpec((1, d), lambda i, j: (0, 0)),
                  wspec(0), wspec(1), wspec(2), wspec(3)],
        out_specs=[ospec, ospec, ospec, ospec],
        out_shape=[jax.ShapeDtypeStruct((m, sbw), BF16),
                   jax.ShapeDtypeStruct((m, sbw), F32),
                   jax.ShapeDtypeStruct((m, sbw), F32),
                   jax.ShapeDtypeStruct((m, pw), F32)],
        scratch_shapes=[pltpu.VMEM((tm, d), BF16)],
        compiler_params=_params("parallel", "arbitrary"),
        name="in_projection",
    )(x, g, w_in, w_in, w_in, w_in)


def _suffix_matrix(tk, copies=1):
    j = np.arange(tk)[:, None]
    s = np.arange(tk)[None, :]
    return jnp.asarray(np.tile((j > s).astype(np.float32), (copies, 1)), BF16)


def _sb_weights(z, valid, u, carry, stack=False):
    r, tk = z.shape
    sp = jnp.maximum(z, 0.0) + jnp.log2(1.0 + jnp.exp2(-jnp.abs(z)))
    if valid is not None:
        sp = jnp.where(valid, sp, 0.0)
    hi, lo = _split_bf16(sp)
    if stack:
        su = _dot(jnp.concatenate([hi, lo], axis=0), u)
        su = su[:r] + su[r:]
    else:
        su = _dot(jnp.concatenate([hi, lo], axis=1), u)
    later = su + jnp.concatenate([carry] * (tk // LANES), axis=1)
    w = jnp.exp2(z - sp - later)
    if valid is not None:
        w = jnp.where(valid, w, 0.0)
    total = jnp.broadcast_to(jnp.sum(sp, axis=1, keepdims=True), (r, LANES))
    return w, carry + total


def _prompt_attn_kernel(bias_ref, q_ref, k_ref, v_ref, km_ref, vm_ref, u1_ref, u1m_ref, *rest, tq, tk, n_cast):
    cast_in = rest[:n_cast]
    o_ref = rest[n_cast]
    cast_out = rest[n_cast + 1:2 * n_cast + 1]
    kb_ref, vb_ref, carry_ref, acc_ref = rest[2 * n_cast + 1:]
    h = pl.program_id(1)
    qi = pl.program_id(2)

    for src, dst in zip(cast_in, cast_out):
        def cast(rows, src=src, dst=dst):
            dst[rows, :] = src[rows, :].astype(BF16)

        _for_row_chunks(src.shape[0], max(16, CAST_CHUNK_ELEMS // src.shape[1]), cast)

    @pl.when(qi == 0)
    def _():
        kb_ref[...] = k_ref[...].astype(BF16)
        vb_ref[...] = v_ref[...].astype(BF16)

    bias = bias_ref[h]
    carry_ref[...] = jnp.zeros_like(carry_ref)
    acc_ref[...] = jnp.zeros_like(acc_ref)

    def block(r0, kblk, vblk, valid, u):
        z = _dot_nt(q_ref[r0:, :], kblk) + bias
        w, carry = _sb_weights(z, valid, u, carry_ref[r0:, :])
        carry_ref[r0:, :] = carry
        acc_ref[r0:, :] += _dot(w.astype(BF16), vblk)

    for d in reversed(range(tq // tk)):
        r0 = d * tk
        off = pl.multiple_of(qi * tq + r0, tk)
        row = lax.broadcasted_iota(jnp.int32, (tq - r0, tk), 0)
        col = lax.broadcasted_iota(jnp.int32, (tq - r0, tk), 1)
        block(r0, kb_ref[pl.ds(off, tk), :], vb_ref[pl.ds(off, tk), :], col < row, u1_ref[...])

    n_before = qi * (tq // tk)

    def body(it, c):
        off = pl.multiple_of((n_before - 1 - it) * tk, tk)
        block(0, kb_ref[pl.ds(off, tk), :], vb_ref[pl.ds(off, tk), :], None, u1_ref[...])
        return c

    lax.fori_loop(0, n_before, body, 0)

    colm = lax.broadcasted_iota(jnp.int32, (tq, LANES), 1)
    block(0, km_ref[...].astype(BF16), vm_ref[...].astype(BF16), colm < N_META, u1m_ref[...])
    o_ref[...] = acc_ref[...].astype(o_ref.dtype)


def _prompt_attention(q, k, v, k_meta, v_meta, bias, batch, seq, cast_along=(), tq_pref=2048, tk_pref=256):
    m, sbw = q.shape
    nh = sbw // HEAD_DIM
    tq = _tile(seq, tq_pref)
    tk = _tile(tq, tk_pref)
    nq = seq // tq
    pad = ((0, LANES - N_META), (0, 0))
    km = jnp.pad(k_meta, pad)
    vm = jnp.pad(v_meta, pad)
    kern = functools.partial(_prompt_attn_kernel, tq=tq, tk=tk, n_cast=len(cast_along))
    qspec = pl.BlockSpec((tq, HEAD_DIM), lambda b, h, i: (b * nq + i, h))
    kvspec = pl.BlockSpec((seq, HEAD_DIM), lambda b, h, i: (b, h))
    mspec = pl.BlockSpec((LANES, HEAD_DIM), lambda b, h, i: (0, h))
    n_steps = batch * nh * nq
    cast_specs = []
    for w in cast_along:
        rows, cols = w.shape
        slab = rows // n_steps
        assert slab * n_steps == rows and slab % 16 == 0
        cast_specs.append(pl.BlockSpec((slab, cols), lambda b, h, i: ((b * nh + h) * nq + i, 0)))
    out = pl.pallas_call(
        kern,
        grid=(batch, nh, nq),
        in_specs=[pl.BlockSpec(memory_space=pltpu.SMEM), qspec, kvspec, kvspec, mspec, mspec,
                  pl.BlockSpec((2 * tk, tk), lambda b, h, i: (0, 0)),
                  pl.BlockSpec((2 * LANES, LANES), lambda b, h, i: (0, 0))] + cast_specs,
        out_specs=[qspec] + cast_specs,
        out_shape=[jax.ShapeDtypeStruct((m, sbw), BF16)] + [jax.ShapeDtypeStruct(w.shape, BF16) for w in cast_along],
        scratch_shapes=[pltpu.VMEM((seq, HEAD_DIM), BF16), pltpu.VMEM((seq, HEAD_DIM), BF16),
                        pltpu.VMEM((tq, LANES), F32), pltpu.VMEM((tq, HEAD_DIM), F32)],
        compiler_params=_params("parallel", "parallel", "arbitrary"),
        name="prompt_attention",
    )(bias, q, k, v, km, vm, _suffix_matrix(tk, 2), _suffix_matrix(LANES, 2), *cast_along)
    return out[0], tuple(out[1:])


HEAD_TILE = SUBLANES
PAGES_PER_BLOCK = 4


def _sample_attn_kernel(pt_ref, q_ref, kn_ref, vn_ref, bias_ref, u1_ref, ug_ref, mask_ref, sel_ref, selt_ref, *rest,
                        group):
    kc_refs = rest[:group]
    vc_refs = rest[group:2 * group]
    o_ref, carry_ref, acc_ref = rest[2 * group:]
    del pt_ref
    j = pl.program_id(1)
    n_half, hr = q_ref.shape[:2]
    bias = bias_ref[...]

    def attend(k_of, v_of, n_blocks, valid, u):
        r = k_of(0, 0).shape[0]
        m = mask_ref[:, :r]
        halves = []
        for half in range(n_half):
            kcat = jnp.concatenate([k_of(i, half) for i in range(n_blocks)], axis=0)
            zt = _dot_nt(q_ref[half], kcat)
            for i in range(n_blocks):
                halves.extend(_split_bf16(zt[:, i * r:(i + 1) * r] * m))
        zz = _dot(jnp.concatenate(halves, axis=0), sel_ref[:r, :])
        zs = []
        for i in range(n_blocks):
            parts = []
            for half in range(n_half):
                base = (half * n_blocks + i) * 2 * hr
                parts.append(zz[base:base + hr] + zz[base + hr:base + 2 * hr])
            zs.append(jnp.concatenate(parts, axis=0) + bias)
        w, carry = _sb_weights(jnp.concatenate(zs, axis=1), valid, u, carry_ref[...], stack=True)
        carry_ref[...] = carry
        w = w.astype(BF16)
        wrows = jnp.concatenate([w[half * hr:(half + 1) * hr, i * LANES:(i + 1) * LANES]
                                 for half in range(n_half) for i in range(n_blocks)], axis=0)
        wide = _dot(wrows, selt_ref[:, :r])
        for half in range(n_half):
            lhs = jnp.concatenate(
                [(wide[(half * n_blocks + i) * hr:(half * n_blocks + i + 1) * hr] * m).astype(BF16)
                 for i in range(n_blocks)], axis=1)
            vcat = jnp.concatenate([v_of(i, half) for i in range(n_blocks)], axis=0)
            acc_ref[half] += _dot(lhs, vcat)

    @pl.when(j == 0)
    def _():
        carry_ref[...] = jnp.zeros_like(carry_ref)
        acc_ref[...] = jnp.zeros_like(acc_ref)
        t = (lax.broadcasted_iota(jnp.int32, (n_half * hr, LANES), 0) % hr) // HEAD_TILE
        key = lax.broadcasted_iota(jnp.int32, (n_half * hr, LANES), 1)
        attend(lambda i, half: kn_ref[half].astype(BF16), lambda i, half: vn_ref[half].astype(BF16), 1,
               key < t, u1_ref[...])

    def half_page(ref, half):
        rows = ref.shape[0] // n_half
        x = ref[pl.ds(half, rows, stride=n_half), :, :]
        return x.reshape(rows * HEAD_TILE, HEAD_DIM).astype(BF16)

    for first in range(0, group, PAGES_PER_BLOCK):
        last = first + PAGES_PER_BLOCK - 1
        attend(lambda i, half: half_page(kc_refs[last - i], half),
               lambda i, half: half_page(vc_refs[last - i], half), PAGES_PER_BLOCK, None, ug_ref[...])

    @pl.when(j == pl.num_programs(1) - 1)
    def _():
        o_ref[...] = acc_ref[...].astype(o_ref.dtype)


def _sample_attention(q, k_new, v_new, cache_k, cache_v, page_table, bias, group_pref=8):
    db, ds, sbw = q.shape
    nh = sbw // HEAD_DIM
    n_pages = page_table.shape[1]
    n_phys, page = cache_k.shape[:2]
    group = _tile(n_pages, group_pref)
    assert page == LANES and ds <= page and nh % HEAD_TILE == 0 and group % PAGES_PER_BLOCK == 0
    n_half = nh // HEAD_TILE
    hr = ds * HEAD_TILE
    flat = page * HEAD_TILE

    def by_half(a):
        a = a.reshape(db, ds, n_half, HEAD_TILE, HEAD_DIM).transpose(0, 2, 1, 3, 4)
        return a.reshape(db, n_half, hr, HEAD_DIM)

    mask = (np.arange(hr)[:, None] % HEAD_TILE == np.arange(flat)[None, :] % HEAD_TILE).astype(np.float32)
    sel = (np.arange(flat)[:, None] // HEAD_TILE == np.arange(LANES)[None, :]).astype(np.float32)
    bias_rows = jnp.broadcast_to(bias.astype(F32).reshape(n_half, 1, HEAD_TILE, 1),
                                 (n_half, ds, HEAD_TILE, LANES)).reshape(n_half * hr, LANES)
    pt = page_table.reshape(-1).astype(jnp.int32)

    def const(shape):
        return pl.BlockSpec(shape, lambda n, j, pt: (0,) * len(shape))

    seq_spec = pl.BlockSpec((None, n_half, hr, HEAD_DIM), lambda n, j, pt: (n, 0, 0, 0))

    def page_spec(c):
        def imap(n, j, pt):
            return (pt[n * n_pages + (n_pages - 1 - (j * group + c))], 0, 0, 0)
        return pl.BlockSpec((None, page * n_half, HEAD_TILE, HEAD_DIM), imap)

    kern = functools.partial(_sample_attn_kernel, group=group)
    ck = cache_k.reshape(n_phys, page * n_half, HEAD_TILE, HEAD_DIM)
    cv = cache_v.reshape(n_phys, page * n_half, HEAD_TILE, HEAD_DIM)
    out = pl.pallas_call(
        kern,
        grid_spec=pltpu.PrefetchScalarGridSpec(
            num_scalar_prefetch=1,
            grid=(db, n_pages // group),
            in_specs=[seq_spec, seq_spec, seq_spec, const((n_half * hr, LANES)), const((page, page)),
                      const((PAGES_PER_BLOCK * page, PAGES_PER_BLOCK * page)),
                      const((hr, flat)), const((flat, LANES)), const((LANES, flat))]
                     + [page_spec(c) for c in range(group)] * 2,
            out_specs=seq_spec,
            scratch_shapes=[pltpu.VMEM((n_half * hr, LANES), F32), pltpu.VMEM((n_half, hr, HEAD_DIM), F32)]),
        out_shape=jax.ShapeDtypeStruct((db, n_half, hr, HEAD_DIM), BF16),
        compiler_params=_params("parallel", "arbitrary"),
        name="sample_attention",
    )(pt, by_half(q), by_half(k_new), by_half(v_new),
      bias_rows, _suffix_matrix(page), _suffix_matrix(PAGES_PER_BLOCK * page), jnp.asarray(mask), jnp.asarray(sel, BF16),
      jnp.asarray(sel.T, BF16), *([ck] * group), *([cv] * group))
    out = out.reshape(db, n_half, ds, HEAD_TILE, HEAD_DIM).transpose(0, 2, 1, 3, 4)
    return out.reshape(db, ds, sbw)


def _pool_tail(b_ref, ext, w_ref, s_ref, o_ref):
    ng, gd = w_ref.shape[:2]
    for g in range(ng):
        cols = slice(g * gd, (g + 1) * gd)
        hi, lo = _split_bf16(ext[:, cols])
        d = _dot(b_ref[g], jnp.concatenate([hi, lo], axis=0))
        o_ref[:, cols] = (_dot(d.astype(BF16), w_ref[g]) * s_ref[:, cols]).astype(o_ref.dtype)


def _pool_prompt_kernel(p_ref, halo_ref, meta_ref, b_ref, w_ref, s_ref, o_ref, *, tiles_per_seq):
    first = (pl.program_id(0) % tiles_per_seq) == 0
    halo = jnp.where(first, meta_ref[...], halo_ref[...])
    _pool_tail(b_ref, jnp.concatenate([halo, p_ref[...]], axis=0), w_ref, s_ref, o_ref)


def _pool_sample_kernel(e_ref, b_ref, w_ref, s_ref, o_ref):
    _pool_tail(b_ref, e_ref[...], w_ref, s_ref, o_ref)


def _window_rows(n_out, n_in, first):
    b = np.zeros((len(POOL_WINDOWS), n_out, n_in), np.float32)
    for g, w in enumerate(POOL_WINDOWS):
        for r in range(n_out):
            b[g, r, first + r - w + 1:first + r + 1] += 1.0 / w
            b[g, r, first + r] -= 1.0
    return b


def _pool_prompt(p, p_meta, w_pool, pool_scale, seq, tm_pref=512):
    m, pw = p.shape
    ng, gd, _ = w_pool.shape
    tm = _tile(seq, tm_pref)
    assert tm % LANES == 0
    hb = tm // LANES
    meta = jnp.pad(p_meta, ((LANES - N_META, 0), (0, 0)))
    band = np.tile(_window_rows(tm, LANES + tm, LANES), (1, 1, 2))
    kern = functools.partial(_pool_prompt_kernel, tiles_per_seq=seq // tm)
    once = pl.Buffered(1)
    return pl.pallas_call(
        kern,
        grid=(m // tm,),
        in_specs=[pl.BlockSpec((tm, pw), lambda i: (i, 0)),
                  pl.BlockSpec((LANES, pw), lambda i: (jnp.maximum(i * hb - 1, 0), 0)),
                  pl.BlockSpec((LANES, pw), lambda i: (0, 0), pipeline_mode=once),
                  pl.BlockSpec(band.shape, lambda i: (0, 0, 0), pipeline_mode=once),
                  pl.BlockSpec((ng, gd, gd), lambda i: (0, 0, 0), pipeline_mode=once),
                  pl.BlockSpec((1, pw), lambda i: (0, 0))],
        out_specs=pl.BlockSpec((tm, pw), lambda i: (i, 0)),
        out_shape=jax.ShapeDtypeStruct((m, pw), BF16),
        compiler_params=_params("parallel"),
        name="pool_prompt",
    )(p, p, meta, jnp.asarray(band, BF16), w_pool, pool_scale)


_SAMPLE_EXT_ROWS = 32


def _pool_sample(state, p_new, w_pool, pool_scale, seqs_pref=16):
    db, ds, pw = p_new.shape
    ng, gd, _ = w_pool.shape
    ns = _tile(db, seqs_pref)
    assert POOL_BUF + ds <= _SAMPLE_EXT_ROWS and ds <= SUBLANES
    ext = jnp.concatenate(
        [state, p_new, jnp.zeros((db, _SAMPLE_EXT_ROWS - POOL_BUF - ds, pw), F32)], axis=1)
    ext = ext.reshape(db * _SAMPLE_EXT_ROWS, pw)
    one = np.zeros((len(POOL_WINDOWS), SUBLANES, _SAMPLE_EXT_ROWS), np.float32)
    one[:, :ds] = _window_rows(ds, _SAMPLE_EXT_ROWS, POOL_BUF)
    band = np.zeros((len(POOL_WINDOWS), ns * SUBLANES, ns * _SAMPLE_EXT_ROWS), np.float32)
    for s in range(ns):
        band[:, s * SUBLANES:(s + 1) * SUBLANES, s * _SAMPLE_EXT_ROWS:(s + 1) * _SAMPLE_EXT_ROWS] = one
    band = np.tile(band, (1, 1, 2))
    once = pl.Buffered(1)
    out = pl.pallas_call(
        _pool_sample_kernel,
        grid=(db // ns,),
        in_specs=[pl.BlockSpec((ns * _SAMPLE_EXT_ROWS, pw), lambda i: (i, 0)),
                  pl.BlockSpec(band.shape, lambda i: (0, 0, 0), pipeline_mode=once),
                  pl.BlockSpec((ng, gd, gd), lambda i: (0, 0, 0), pipeline_mode=once),
                  pl.BlockSpec((1, pw), lambda i: (0, 0))],
        out_specs=pl.BlockSpec((ns * SUBLANES, pw), lambda i: (i, 0)),
        out_shape=jax.ShapeDtypeStruct((db * SUBLANES, pw), BF16),
        compiler_params=_params("parallel"),
        name="pool_sample",
    )(ext, jnp.asarray(band, BF16), w_pool, pool_scale)
    return out.reshape(db, SUBLANES, pw)[:, :ds]


def _accumulate(o_ref, terms, assign=False, col_chunk=1024):
    n = o_ref.shape[1]
    cn = min(n, col_chunk)
    for c in range(n // cn):
        cols = slice(c * cn, (c + 1) * cn)
        part = _dot(terms[0][0][...], terms[0][1][:, cols])
        for a_ref, w_ref in terms[1:]:
            part += _dot(a_ref[...], w_ref[:, cols])
        if assign:
            o_ref[:, cols] = part
        else:
            o_ref[:, cols] += part


def _epilogue_rows(step, step_rows, chunk, fn):
    chunk = _tile(step_rows, chunk)

    def body(c, carry):
        local = pl.multiple_of(c * chunk, chunk)
        fn(pl.ds(local, chunk), pl.ds(pl.multiple_of(step * step_rows, step_rows) + local, chunk))
        return carry

    lax.fori_loop(0, step_rows // chunk, body, 0)


def _outproj_kernel(a1_ref, a2_ref, w1_ref, w2_ref, x_ref, gpost_ref, gpre_ref, h_ref, hn_ref, acc_ref, *, nk):
    s = pl.program_id(1)

    @pl.when(s == 0)
    def _():
        _accumulate(acc_ref, [(a1_ref, w1_ref), (a2_ref, w2_ref)], assign=True)

    @pl.when(jnp.logical_and(s > 0, s < nk))
    def _():
        _accumulate(acc_ref, [(a1_ref, w1_ref), (a2_ref, w2_ref)])

    @pl.when(s >= nk)
    def _():
        gpost = gpost_ref[...]
        gpre = gpre_ref[...]

        def finish(rows, acc_rows):
            h = x_ref[rows, :] + _rms(acc_ref[acc_rows, :], gpost)
            h_ref[rows, :] = h
            hn_ref[rows, :] = _rms(h, gpre).astype(BF16)

        _epilogue_rows(s - nk, x_ref.shape[0], 64, finish)


def _out_projection(a1, a2, w_out, x, g_post, g_pre, tm_pref=1024, tk_pref=512):
    m, k1 = a1.shape
    d = w_out.shape[1]
    tm = _tile(m, tm_pref)
    tk = _tile(k1, tk_pref)
    nk = k1 // tk
    er = _tile(tm, EPILOGUE_ROWS)
    ne = tm // er

    def kstep(s):
        return jnp.minimum(s, nk - 1)

    def erow(i, s):
        return i * ne + jnp.maximum(s - nk, 0)

    return pl.pallas_call(
        functools.partial(_outproj_kernel, nk=nk),
        grid=(m // tm, nk + ne),
        in_specs=[pl.BlockSpec((tm, tk), lambda i, s: (i, kstep(s))),
                  pl.BlockSpec((tm, tk), lambda i, s: (i, kstep(s))),
                  pl.BlockSpec((tk, d), lambda i, s: (kstep(s), 0)),
                  pl.BlockSpec((tk, d), lambda i, s: (kstep(s) + nk, 0)),
                  pl.BlockSpec((er, d), lambda i, s: (erow(i, s), 0)),
                  pl.BlockSpec((1, d), lambda i, s: (0, 0)),
                  pl.BlockSpec((1, d), lambda i, s: (0, 0))],
        out_specs=[pl.BlockSpec((er, d), lambda i, s: (erow(i, s), 0)),
                   pl.BlockSpec((er, d), lambda i, s: (erow(i, s), 0))],
        out_shape=[jax.ShapeDtypeStruct((m, d), F32), jax.ShapeDtypeStruct((m, d), BF16)],
        scratch_shapes=[pltpu.VMEM((tm, d), F32)],
        compiler_params=_params("parallel", "arbitrary"),
        name="out_projection",
    )(a1, a2, w_out, w_out, x, g_post, g_pre)


def _up_kernel(hn_ref, w_ref, o_ref):
    u = jnp.maximum(_dot(hn_ref[...], w_ref[...]), 0.0)
    o_ref[...] = (u * u).astype(o_ref.dtype)


def _mlp_up(hn, w_up, tm_pref=1024, tn_pref=1024):
    m, d = hn.shape
    f = w_up.shape[1]
    tm = _tile(m, tm_pref)
    tn = _tile(f, tn_pref)
    return pl.pallas_call(
        _up_kernel,
        grid=(m // tm, f // tn),
        in_specs=[pl.BlockSpec((tm, d), lambda i, j: (i, 0)),
                  pl.BlockSpec((d, tn), lambda i, j: (0, j))],
        out_specs=pl.BlockSpec((tm, tn), lambda i, j: (i, j)),
        out_shape=jax.ShapeDtypeStruct((m, f), BF16),
        compiler_params=_params("parallel", "arbitrary"),
        name="mlp_up",
    )(hn, w_up)


def _down_kernel(u_ref, w_ref, h_ref, g_ref, y_ref, acc_ref, *, nk):
    s = pl.program_id(1)

    @pl.when(s == 0)
    def _():
        _accumulate(acc_ref, [(u_ref, w_ref)], assign=True)

    @pl.when(jnp.logical_and(s > 0, s < nk))
    def _():
        _accumulate(acc_ref, [(u_ref, w_ref)])

    @pl.when(s >= nk)
    def _():
        g = g_ref[...]

        def finish(rows, acc_rows):
            y_ref[rows, :] = h_ref[rows, :] + _rms(acc_ref[acc_rows, :], g)

        _epilogue_rows(s - nk, h_ref.shape[0], 64, finish)


def _mlp_down(u2, w_down, h, g_post, tm_pref=1024, tk_pref=1024):
    m, f = u2.shape
    d = w_down.shape[1]
    tm = _tile(m, tm_pref)
    tk = _tile(f, tk_pref)
    nk = f // tk
    er = _tile(tm, EPILOGUE_ROWS)
    ne = tm // er

    def kstep(s):
        return jnp.minimum(s, nk - 1)

    def erow(i, s):
        return i * ne + jnp.maximum(s - nk, 0)

    return pl.pallas_call(
        functools.partial(_down_kernel, nk=nk),
        grid=(m // tm, nk + ne),
        in_specs=[pl.BlockSpec((tm, tk), lambda i, s: (i, kstep(s))),
                  pl.BlockSpec((tk, d), lambda i, s: (kstep(s), 0)),
                  pl.BlockSpec((er, d), lambda i, s: (erow(i, s), 0)),
                  pl.BlockSpec((1, d), lambda i, s: (0, 0))],
        out_specs=pl.BlockSpec((er, d), lambda i, s: (erow(i, s), 0)),
        out_shape=jax.ShapeDtypeStruct((m, d), F32),
        scratch_shapes=[pltpu.VMEM((tm, d), F32)],
        compiler_params=_params("parallel", "arbitrary"),
        name="mlp_down",
    )(u2, w_down, h, g_post)


def _finish_layer(x, attn, pool, w_out, g_mix_post, g_mlp_pre, w_up, w_down, g_mlp_post):
    h, hn = _out_projection(attn, pool, w_out, x, g_mix_post, g_mlp_pre)
    u2 = _mlp_up(hn, w_up)
    return _mlp_down(u2, w_down, h, g_mlp_post)


def kernel(x_prompt, x_sample, cache_k, cache_v, state_pool, page_table, meta, g_mix_pre, w_in, sb_bias,
           w_pool, pool_scale, w_out, g_mix_post, g_mlp_pre, w_up, w_down, g_mlp_post):
    depth = w_in.shape[0]
    assert depth == 1, "meta rows skip the MLP, which is only valid for a single layer"
    b, s, d = x_prompt.shape
    db, ds, _ = x_sample.shape
    nh = cache_k.shape[3]
    sbw = nh * HEAD_DIM
    pw = w_pool.shape[1] * w_pool.shape[2]
    layer = 0

    w_in_b = w_in[layer].astype(BF16)
    w_pool_b = w_pool[layer].astype(BF16)
    g_pre = g_mix_pre[layer][None, :]
    scale_row = pool_scale[layer][None, :]
    bias = sb_bias[layer].astype(F32) * LOG2E

    xp = x_prompt.reshape(b * s, d)
    q_p, k_p, v_p, p_p = _in_projection(xp, g_pre, w_in_b, sbw, pw)
    xs = x_sample.reshape(db * ds, d)
    n_s = db * ds
    q_sm, k_sm, v_sm, p_sm = _in_projection(jnp.concatenate([xs, meta.astype(F32)], axis=0), g_pre, w_in_b, sbw, pw,
                                            tm_pref=n_s + N_META)
    q_s, k_s, v_s, p_s = q_sm[:n_s], k_sm[:n_s], v_sm[:n_s], p_sm[:n_s]
    k_m, v_m, p_m = k_sm[n_s:], v_sm[n_s:], p_sm[n_s:]

    attn_p, (w_out_b, w_up_b, w_down_b) = _prompt_attention(
        q_p, k_p, v_p, k_m, v_m, bias, b, s, cast_along=(w_out[layer], w_up[layer], w_down[layer]))
    mlp = (w_out_b, g_mix_post[layer][None, :], g_mlp_pre[layer][None, :], w_up_b, w_down_b,
           g_mlp_post[layer][None, :])
    pool_p = _pool_prompt(p_p, p_m, w_pool_b, scale_row, s)
    y_p = _finish_layer(xp, attn_p, pool_p, *mlp)

    attn_s = _sample_attention(q_s.reshape(db, ds, sbw), k_s.reshape(db, ds, sbw), v_s.reshape(db, ds, sbw),
                               cache_k[layer], cache_v[layer], page_table, bias)
    pool_s = _pool_sample(state_pool[layer], p_s.reshape(db, ds, pw), w_pool_b, scale_row)
    y_s = _finish_layer(xs, attn_s.reshape(n_s, sbw), pool_s.reshape(n_s, pw), *mlp)

    def with_meta(a_meta, a):
        full = jnp.concatenate([jnp.broadcast_to(a_meta[None], (b, N_META, sbw)), a.reshape(b, s, sbw)], axis=1)
        return full.reshape(1, b, N_META + s, nh, HEAD_DIM)

    assert s >= POOL_BUF
    pool_prompt = p_p.reshape(b, s, pw)[:, -POOL_BUF:]
    pool_sample = jnp.concatenate([state_pool[layer], p_s.reshape(db, ds, pw)], axis=1)[:, -POOL_BUF:]
    return (y_p.reshape(b, s, d), y_s.reshape(db, ds, d),
            with_meta(k_m, k_p), with_meta(v_m, v_p), pool_prompt[None],
            k_s.reshape(1, db, ds, nh, HEAD_DIM), v_s.reshape(1, db, ds, nh, HEAD_DIM), pool_sample[None])
```
